```python
import jax
import jax.numpy as jnp
from jax import lax
import numpy as np


D_MODEL = 4096
BATCH = 2
SEQ = 4096
DEPTH = 4

CHUNK = 64
N_GROUPS = 4
GROUP_WIDTH = D_MODEL // N_GROUPS

HGRN_HEADS = 8
HGRN_KEY = GROUP_WIDTH // HGRN_HEADS
HGRN_VAL = GROUP_WIDTH // HGRN_HEADS

LRU_WIDTH = GROUP_WIDTH
LRU_BLOCKS = 8
LRU_BLOCK = LRU_WIDTH // LRU_BLOCKS
CONV_WIDTH = 4
LRU_C = 8.0

GLA_HEADS = 4
GLA_KEY = GROUP_WIDTH // 2 // GLA_HEADS
GLA_VAL = GROUP_WIDTH // GLA_HEADS
GLA_RANK = 16
GLA_NORMALIZER = 16.0

RET_HEADS = 8
RET_KEY = GROUP_WIDTH // RET_HEADS
RET_VAL = GROUP_WIDTH // RET_HEADS
ROPE_BASE = 10000.0

D_FF = 2 * D_MODEL
N_EXPERTS = 8
TOP_K = 2
D_FF_EXPERT = D_MODEL // 2
N_DENSE = (DEPTH + 1) // 2
N_MOE = DEPTH // 2

DEEPNORM_ALPHA = (2 * DEPTH) ** 0.25
DEEPNORM_BETA = (8 * DEPTH) ** -0.25
LN_EPS = 1e-5
RMS_EPS = 1e-6

IN_SPLITS = (
    HGRN_HEADS * HGRN_KEY, HGRN_HEADS * HGRN_KEY, HGRN_HEADS * HGRN_VAL, HGRN_HEADS * HGRN_VAL,
    LRU_WIDTH, LRU_WIDTH,
    GLA_HEADS * GLA_KEY, GLA_HEADS * GLA_KEY, GLA_HEADS * GLA_VAL, GLA_HEADS * GLA_VAL, GLA_RANK,
    RET_HEADS * RET_KEY, RET_HEADS * RET_KEY, RET_HEADS * RET_VAL, RET_HEADS * RET_VAL,
)
D_IN = sum(IN_SPLITS)

kernel_name = 'hybrid_parallel_recurrent_mixers_deepnorm_moe'


def layer_norm(x, g, b):
    xf = x.astype(jnp.float32)
    mu = jnp.mean(xf, axis=-1, keepdims=True)
    var = jnp.mean(jnp.square(xf - mu), axis=-1, keepdims=True)
    return ((xf - mu) * lax.rsqrt(var + LN_EPS) * g + b).astype(x.dtype)


def rms_norm(x, g):
    return x * lax.rsqrt(jnp.mean(jnp.square(x), axis=-1, keepdims=True) + RMS_EPS) * g


def group_layer_norm(x, g):
    mu = jnp.mean(x, axis=-1, keepdims=True)
    var = jnp.mean(jnp.square(x - mu), axis=-1, keepdims=True)
    return (x - mu) * lax.rsqrt(var + LN_EPS) * g


def split_heads(a, n_heads):
    b, t, _ = a.shape
    return a.reshape(b, t, n_heads, -1).transpose(0, 2, 1, 3)


def merge_heads(a):
    b, h, t, d = a.shape
    return a.transpose(0, 2, 1, 3).reshape(b, t, h * d)


def to_chunks(a):
    b, h, t, d = a.shape
    return a.reshape(b, h, t // CHUNK, CHUNK, d)


def gla_chunk_scan(q, k, v, log_g):
    b, h, t, kd = q.shape
    vd = v.shape[-1]
    xs = tuple(jnp.moveaxis(to_chunks(a.astype(jnp.float32)), 2, 0) for a in (q, k, v, log_g))
    causal = jnp.tril(jnp.ones((CHUNK, CHUNK), dtype=bool))

    def step(state, inp):
        q_c, k_c, v_c, g_c = inp
        cum = jnp.cumsum(g_c, axis=-2)
        diff = cum[..., :, None, :] - cum[..., None, :, :]
        decay = jnp.exp(jnp.where(causal[:, :, None], diff, -jnp.inf))
        scores = jnp.einsum('bhtk,bhsk,bhtsk->bhts', q_c, k_c, decay)
        out = jnp.einsum('bhts,bhsv->bhtv', scores, v_c) + jnp.einsum('bhtk,bhkv->bhtv', q_c * jnp.exp(cum), state)
        last = cum[..., -1:, :]
        state = jnp.exp(last[..., 0, :])[..., None] * state + jnp.einsum('bhsk,bhsv->bhkv', k_c * jnp.exp(last - cum), v_c)
        return state, out

    _, o = lax.scan(step, jnp.zeros((b, h, kd, vd), jnp.float32), xs)
    return jnp.moveaxis(o, 0, 2).reshape(b, h, t, vd)


def rotary(x):
    t, d = x.shape[-2], x.shape[-1]
    half = d // 2
    inv_freq = ROPE_BASE ** (-jnp.arange(half, dtype=jnp.float32) / half)
    ang = jnp.arange(t, dtype=jnp.float32)[:, None] * inv_freq[None, :]
    cos, sin = jnp.cos(ang), jnp.sin(ang)
    x1, x2 = x[..., :half], x[..., half:]
    return jnp.concatenate([x1 * cos - x2 * sin, x1 * sin + x2 * cos], axis=-1)


def retention_chunkwise(q, k, v, log_gamma):
    b, h, t, kd = q.shape
    vd = v.shape[-1]
    qc, kc, vc = to_chunks(q), to_chunks(k), to_chunks(v)
    pos = jnp.arange(CHUNK, dtype=jnp.float32)
    rel = pos[:, None] - pos[None, :]
    intra = jnp.where(rel >= 0, jnp.exp(log_gamma[:, None, None] * jnp.maximum(rel, 0.0)), 0.0)
    scores = jnp.einsum('bhntk,bhnsk->bhnts', qc, kc) * intra[None, :, None]
    o_intra = jnp.einsum('bhnts,bhnsv->bhntv', scores, vc)
    k_decay = jnp.exp(log_gamma[:, None] * (CHUNK - 1.0 - pos)[None, :])
    chunk_kv = jnp.einsum('bhnsk,hs,bhnsv->nbhkv', kc, k_decay, vc)
    chunk_decay = jnp.exp(log_gamma * CHUNK)[None, :, None, None]

    def step(state, kv):
        return chunk_decay * state + kv, state

    _, states_before = lax.scan(step, jnp.zeros((b, h, kd, vd), jnp.float32), chunk_kv)
    q_decay = jnp.exp(log_gamma[:, None] * (pos + 1.0)[None, :])
    o_inter = jnp.einsum('bhntk,ht,nbhkv->bhntv', qc, q_decay, states_before)
    return (o_intra + o_inter).reshape(b, h, t, vd)


def causal_depthwise_conv(x, w, bias):
    y = lax.conv_general_dilated(x, w.astype(jnp.float32)[:, None, :], window_strides=(1,),
                                 padding=[(CONV_WIDTH - 1, 0)], dimension_numbers=('NWC', 'WIO', 'NWC'),
                                 feature_group_count=x.shape[-1])
    return y + bias


def rg_lru(x, w_a, b_a, w_x, b_x, lam):
    b, t, w = x.shape
    xb = x.reshape(b, t, LRU_BLOCKS, LRU_BLOCK)
    r = jax.nn.sigmoid(jnp.einsum('btni,nij->btnj', xb, w_a).reshape(b, t, w) + b_a)
    i = jax.nn.sigmoid(jnp.einsum('btni,nij->btnj', xb, w_x).reshape(b, t, w) + b_x)
    log_a = -LRU_C * r * jax.nn.softplus(-lam)
    a = jnp.exp(log_a)
    u = jnp.sqrt(-jnp.expm1(2.0 * log_a)) * (i * x)

    def combine(left, right):
        a_l, u_l = left
        a_r, u_r = right
        return a_l * a_r, a_r * u_l + u_r

    _, hs = lax.associative_scan(combine, (a, u), axis=1)
    return hs


def hybrid_mixer(h, lower_bound, w_in, w_out, hgrn_norm, conv_w, conv_b, lru_wa, lru_ba, lru_wx, lru_bx,
                 lru_lambda, gla_w_gate, gla_b_gate, gla_norm, ret_norm):
    proj = jnp.einsum('btd,de->bte', h, w_in).astype(jnp.float32)
    idx = [int(i) for i in np.cumsum(IN_SPLITS)[:-1]]
    (a_q, a_f, a_i, a_g, b_x, b_y, c_q, c_k, c_v, c_g, c_r,
     d_q, d_k, d_v, d_g) = jnp.split(proj, idx, axis=-1)

    f = lower_bound + (1.0 - lower_bound) * jax.nn.sigmoid(a_f)
    o_a = gla_chunk_scan(split_heads(jax.nn.silu(a_q), HGRN_HEADS) * HGRN_KEY ** -0.5,
                         split_heads(1.0 - f, HGRN_HEADS), split_heads(a_i, HGRN_HEADS),
                         split_heads(jnp.log(f), HGRN_HEADS))
    o_a = merge_heads(rms_norm(o_a, hgrn_norm)) * jax.nn.silu(a_g)

    x_conv = causal_depthwise_conv(b_x, conv_w, conv_b)
    o_b = rg_lru(x_conv, lru_wa, lru_ba, lru_wx, lru_bx, lru_lambda) * jax.nn.gelu(b_y)

    log_alpha = jax.nn.log_sigmoid(jnp.matmul(c_r, gla_w_gate) + gla_b_gate) / GLA_NORMALIZER
    o_c = gla_chunk_scan(split_heads(c_q, GLA_HEADS) * GLA_KEY ** -0.5, split_heads(c_k, GLA_HEADS),
                         split_heads(c_v, GLA_HEADS), split_heads(log_alpha, GLA_HEADS))
    o_c = merge_heads(rms_norm(o_c, gla_norm)) * jax.nn.silu(c_g)

    log_gamma = jnp.log1p(-(2.0 ** (-5.0 - jnp.arange(RET_HEADS, dtype=jnp.float32))))
    q_r = rotary(split_heads(d_q, RET_HEADS))
    k_r = rotary(split_heads(d_k, RET_HEADS)) * RET_KEY ** -0.5
    o_d = retention_chunkwise(q_r, k_r, split_heads(d_v, RET_HEADS), log_gamma)
    o_d = merge_heads(group_layer_norm(o_d, ret_norm)) * jax.nn.silu(d_g)

    mixed = jnp.concatenate([o_a, o_b, o_c, o_d], axis=-1)
    return jnp.einsum('bte,ed->btd', mixed, w_out).astype(h.dtype)


def swiglu(h, w_gate, w_up, w_down):
    return jnp.matmul(jax.nn.silu(jnp.matmul(h, w_gate)) * jnp.matmul(h, w_up), w_down)


def moe_swiglu(h, router_w, w_gate, w_up, w_down):
    b, t, d = h.shape
    tok = h.reshape(b * t, d)
    logits = jnp.matmul(tok, router_w).astype(jnp.float32)
    top_logits, top_idx = lax.top_k(logits, TOP_K)
    top_w = jax.nn.softmax(top_logits, axis=-1)
    gates = jnp.einsum('tk,tke->te', top_w, jax.nn.one_hot(top_idx, N_EXPERTS, dtype=jnp.float32))
    hid = jax.nn.silu(jnp.einsum('td,edf->tef', tok, w_gate)) * jnp.einsum('td,edf->tef', tok, w_up)
    y = jnp.einsum('tef,te,efd->td', hid, gates.astype(hid.dtype), w_down)
    return y.reshape(b, t, d).astype(h.dtype)


def setup_inputs(seed: int = 0) -> dict:
    key = jax.random.key(seed)
    ks = jax.random.split(key, 32)

    def nrm(k, shape, scale):
        return jax.random.normal(k, shape, jnp.float32) * scale

    u = jax.random.uniform(ks[10], (DEPTH, LRU_WIDTH), jnp.float32, minval=0.9, maxval=0.999)
    p = u ** (1.0 / LRU_C)
    return {
        'x': nrm(ks[0], (BATCH, SEQ, D_MODEL), 1.0),
        'w_in': nrm(ks[1], (DEPTH, D_MODEL, D_IN), D_MODEL ** -0.5),
        'w_out': nrm(ks[2], (DEPTH, D_MODEL, D_MODEL), DEEPNORM_BETA * D_MODEL ** -0.5),
        'hgrn_lb': 1.0 + nrm(ks[3], (DEPTH, HGRN_HEADS * HGRN_KEY), 0.1),
        'hgrn_norm': 1.0 + nrm(ks[4], (HGRN_VAL,), 0.02) + jnp.zeros((DEPTH, 1), jnp.float32) + nrm(ks[26], (DEPTH, HGRN_VAL), 0.02),
        'conv_w': nrm(ks[5], (DEPTH, CONV_WIDTH, LRU_WIDTH), CONV_WIDTH ** -0.5),
        'conv_b': nrm(ks[6], (DEPTH, LRU_WIDTH), 0.02),
        'lru_wa': nrm(ks[7], (DEPTH, LRU_BLOCKS, LRU_BLOCK, LRU_BLOCK), LRU_BLOCK ** -0.5),
        'lru_ba': nrm(ks[8], (DEPTH, LRU_WIDTH), 0.02),
        'lru_wx': nrm(ks[9], (DEPTH, LRU_BLOCKS, LRU_BLOCK, LRU_BLOCK), LRU_BLOCK ** -0.5),
        'lru_bx': nrm(ks[11], (DEPTH, LRU_WIDTH), 0.02),
        'lru_lambda': jnp.log(p) - jnp.log1p(-p),
        'gla_w_gate': nrm(ks[12], (DEPTH, GLA_RANK, GLA_HEADS * GLA_KEY), GLA_RANK ** -0.5),
        'gla_b_gate': nrm(ks[13], (DEPTH, GLA_HEADS * GLA_KEY), 0.02),
        'gla_norm': 1.0 + nrm(ks[14], (DEPTH, GLA_VAL), 0.02),
        'ret_norm': 1.0 + nrm(ks[15], (DEPTH, RET_VAL), 0.02),
        'ln1_g': 1.0 + nrm(ks[16], (DEPTH, D_MODEL), 0.02),
        'ln1_b': nrm(ks[17], (DEPTH, D_MODEL), 0.02),
        'ln2_g': 1.0 + nrm(ks[18], (DEPTH, D_MODEL), 0.02),
        'ln2_b': nrm(ks[19], (DEPTH, D_MODEL), 0.02),
        'ffn_w_gate': nrm(ks[20], (N_DENSE, D_MODEL, D_FF), D_MODEL ** -0.5),
        'ffn_w_up': nrm(ks[21], (N_DENSE, D_MODEL, D_FF), D_MODEL ** -0.5),
        'ffn_w_down': nrm(ks[22], (N_DENSE, D_FF, D_MODEL), DEEPNORM_BETA * D_FF ** -0.5),
        'router_w': nrm(ks[23], (N_MOE, D_MODEL, N_EXPERTS), D_MODEL ** -0.5),
        'exp_w_gate': nrm(ks[24], (N_MOE, N_EXPERTS, D_MODEL, D_FF_EXPERT), D_MODEL ** -0.5),
        'exp_w_up': nrm(ks[25], (N_MOE, N_EXPERTS, D_MODEL, D_FF_EXPERT), D_MODEL ** -0.5),
        'exp_w_down': nrm(ks[27], (N_MOE, N_EXPERTS, D_FF_EXPERT, D_MODEL), DEEPNORM_BETA * D_FF_EXPERT ** -0.5),
    }


def reference(x, w_in, w_out, hgrn_lb, hgrn_norm, conv_w, conv_b, lru_wa, lru_ba, lru_wx, lru_bx, lru_lambda,
              gla_w_gate, gla_b_gate, gla_norm, ret_norm, ln1_g, ln1_b, ln2_g, ln2_b, ffn_w_gate, ffn_w_up,
              ffn_w_down, router_w, exp_w_gate, exp_w_up, exp_w_down):
    lb_soft = jax.nn.softmax(hgrn_lb.astype(jnp.float32), axis=0)
    lower_bounds = jnp.cumsum(lb_soft, axis=0) - lb_soft[0]
    h = x
    for layer in range(DEPTH):
        mix = hybrid_mixer(h, lower_bounds[layer], w_in[layer], w_out[layer], hgrn_norm[layer], conv_w[layer],
                           conv_b[layer], lru_wa[layer], lru_ba[layer], lru_wx[layer], lru_bx[layer],
                           lru_lambda[layer], gla_w_gate[layer], gla_b_gate[layer], gla_norm[layer], ret_norm[layer])
        h = layer_norm(DEEPNORM_ALPHA * h + mix, ln1_g[layer], ln1_b[layer])
        j = layer // 2
        if layer % 2 == 0:
            ff = swiglu(h, ffn_w_gate[j], ffn_w_up[j], ffn_w_down[j])
        else:
            ff = moe_swiglu(h, router_w[j], exp_w_gate[j], exp_w_up[j], exp_w_down[j])
        h = layer_norm(DEEPNORM_ALPHA * h + ff, ln2_g[layer], ln2_b[layer])
    return h
```

```python
import functools
import math

import jax
import jax.numpy as jnp
import numpy as np
from jax import lax
from jax.experimental import pallas as pl
from jax.experimental.pallas import tpu as pltpu

F32 = jnp.float32
BF16 = jnp.bfloat16

LANES = 128
SUBLANES = 8
VMEM_LIMIT = 56 * 1024 * 1024

D_MODEL = 4096
N_GROUPS = 4
GROUP_WIDTH = D_MODEL // N_GROUPS
HGRN_HEADS = 8
LRU_BLOCKS = 8
CONV_WIDTH = 4
LRU_C = 8.0
GLA_HEADS = 4
GLA_KEY = 128
GLA_VAL = 256
GLA_RANK = 16
GLA_NORMALIZER = 16.0
RET_HEADS = 8
ROPE_BASE = 10000.0
N_EXPERTS = 8
LN_EPS = 1e-5
RMS_EPS = 1e-6

CB_A_Q, CB_A_F, CB_A_I, CB_A_G = 0, 8, 16, 24
CB_B_X, CB_B_Y = 32, 40
CB_C_Q, CB_C_K, CB_C_V, CB_C_G = 48, 52, 56, 64
CB_D_Q, CB_D_K, CB_D_V, CB_D_G = 72, 80, 88, 96
CB_C_R = 104
D_IN_PADDED = 108 * LANES
GATE_COL0 = 9216

MIX_CHUNK = 128
MIX_TT = 512
SCAN_CHUNK = 64


def _params(semantics):
    return pltpu.CompilerParams(dimension_semantics=semantics, vmem_limit_bytes=VMEM_LIMIT)


def _sigmoid(x):
    return 1.0 / (1.0 + jnp.exp(-x))


def _silu(x):
    return x * _sigmoid(x)


def _softplus(x):
    return jnp.maximum(x, 0.0) + jnp.log1p(jnp.exp(-jnp.abs(x)))


def _dot(a, b):
    return jnp.dot(a, b, preferred_element_type=F32)


def _dot_nt(a, b):
    return lax.dot_general(a, b, (((1,), (1,)), ((), ())), preferred_element_type=F32)


def _dot_tn(a, b):
    return lax.dot_general(a, b, (((0,), (0,)), ((), ())), preferred_element_type=F32)


def _matmul_kernel(a_ref, b_ref, o_ref, *scratch, nk):
    if nk == 1:
        o_ref[...] = _dot(a_ref[...], b_ref[...]).astype(o_ref.dtype)
        return
    acc_ref, = scratch
    k = pl.program_id(2)

    @pl.when(k == 0)
    def _():
        acc_ref[...] = jnp.zeros_like(acc_ref)

    acc_ref[...] += _dot(a_ref[...], b_ref[...])

    @pl.when(k == nk - 1)
    def _():
        o_ref[...] = acc_ref[...].astype(o_ref.dtype)


def _matmul(a, b, *, tm, tn, tk=None, out_dtype, name):
    m, kdim = a.shape
    _, n = b.shape
    tk = kdim if tk is None else tk
    assert m % tm == 0 and n % tn == 0 and kdim % tk == 0
    nk = kdim // tk
    return pl.pallas_call(
        functools.partial(_matmul_kernel, nk=nk),
        grid=(m // tm, n // tn, nk),
        in_specs=[pl.BlockSpec((tm, tk), lambda i, j, k: (i, k)),
                  pl.BlockSpec((tk, tn), lambda i, j, k: (k, j))],
        out_specs=pl.BlockSpec((tm, tn), lambda i, j, k: (i, j)),
        out_shape=jax.ShapeDtypeStruct((m, n), out_dtype),
        scratch_shapes=[pltpu.VMEM((tm, tn), F32)] if nk > 1 else [],
        compiler_params=_params(("parallel", "parallel", "arbitrary")),
        name=name,
    )(a, b)


def _swiglu_up_kernel(x_ref, wg_ref, wu_ref, *rest, gated):
    if gated:
        gate_ref, o_ref = rest
    else:
        o_ref, = rest
    x = x_ref[...]
    g = _dot(x, wg_ref[...])
    u = _dot(x, wu_ref[...])
    hid = _silu(g) * u
    if gated:
        reps = hid.shape[1] // LANES
        hid = hid * jnp.concatenate([gate_ref[...]] * reps, axis=1)
    o_ref[...] = hid.astype(o_ref.dtype)


def _swiglu_up(x, wg, wu, *, tm, tn):
    m, d = x.shape
    f = wg.shape[1]
    return pl.pallas_call(
        functools.partial(_swiglu_up_kernel, gated=False),
        grid=(m // tm, f // tn),
        in_specs=[pl.BlockSpec((tm, d), lambda i, j: (i, 0)),
                  pl.BlockSpec((d, tn), lambda i, j: (0, j)),
                  pl.BlockSpec((d, tn), lambda i, j: (0, j))],
        out_specs=pl.BlockSpec((tm, tn), lambda i, j: (i, j)),
        out_shape=jax.ShapeDtypeStruct((m, f), BF16),
        compiler_params=_params(("parallel", "parallel")),
        name="swiglu_up",
    )(x, wg, wu)


def _moe_up(x, wg, wu, gates_rep, *, tm, tn):
    m, d = x.shape
    n_exp, _, f = wg.shape
    nj = f // tn
    return pl.pallas_call(
        functools.partial(_swiglu_up_kernel, gated=True),
        grid=(m // tm, n_exp, nj),
        in_specs=[pl.BlockSpec((tm, d), lambda i, e, j: (i, 0)),
                  pl.BlockSpec((None, d, tn), lambda i, e, j: (e, 0, j)),
                  pl.BlockSpec((None, d, tn), lambda i, e, j: (e, 0, j)),
                  pl.BlockSpec((None, tm, LANES), lambda i, e, j: (e, i, 0))],
        out_specs=pl.BlockSpec((tm, tn), lambda i, e, j: (i, e * nj + j)),
        out_shape=jax.ShapeDtypeStruct((m, n_exp * f), BF16),
        compiler_params=_params(("parallel", "parallel", "parallel")),
        name="moe_up",
    )(x, wg, wu, gates_rep)


def _router_kernel(h_ref, w_ref, o_ref):
    logits = jnp.dot(h_ref[...], w_ref[...], preferred_element_type=F32,
                     precision=lax.Precision.HIGHEST)
    lane = lax.broadcasted_iota(jnp.int32, logits.shape, 1)
    neg = jnp.float32(-jnp.inf)
    logits = jnp.where(lane < N_EXPERTS, logits, neg)
    m1 = jnp.max(logits, axis=1, keepdims=True)
    i1 = jnp.min(jnp.where(logits == m1, lane, LANES), axis=1, keepdims=True)
    rest = jnp.where(lane == i1, neg, logits)
    m2 = jnp.max(rest, axis=1, keepdims=True)
    i2 = jnp.min(jnp.where(rest == m2, lane, LANES), axis=1, keepdims=True)
    e2 = jnp.exp(m2 - m1)
    w1 = 1.0 / (1.0 + e2)
    w2 = e2 / (1.0 + e2)
    for e in range(N_EXPERTS):
        col = jnp.where(i1 == e, w1, 0.0) + jnp.where(i2 == e, w2, 0.0)
        o_ref[e] = jnp.broadcast_to(col, (col.shape[0], LANES))


def _router(h, router_w_padded, *, tm):
    m, d = h.shape
    return pl.pallas_call(
        _router_kernel,
        grid=(m // tm,),
        in_specs=[pl.BlockSpec((tm, d), lambda i: (i, 0)),
                  pl.BlockSpec((d, LANES), lambda i: (0, 0))],
        out_specs=pl.BlockSpec((N_EXPERTS, tm, LANES), lambda i: (0, i, 0)),
        out_shape=jax.ShapeDtypeStruct((N_EXPERTS, m, LANES), F32),
        compiler_params=_params(("parallel",)),
        name="router",
    )(h, router_w_padded)


def _add_ln_kernel(h_ref, y_ref, g_ref, b_ref, o_ref, ob_ref, *, alpha):
    x = alpha * h_ref[...] + y_ref[...]
    mu = jnp.mean(x, axis=-1, keepdims=True)
    xc = x - mu
    var = jnp.mean(xc * xc, axis=-1, keepdims=True)
    out = xc * lax.rsqrt(var + LN_EPS) * g_ref[...] + b_ref[...]
    o_ref[...] = out
    ob_ref[...] = out.astype(BF16)


def _add_ln(h, y, g, b, *, alpha, tm):
    m, d = h.shape
    row = pl.BlockSpec((tm, d), lambda i: (i, 0))
    vec = pl.BlockSpec((1, d), lambda i: (0, 0))
    return pl.pallas_call(
        functools.partial(_add_ln_kernel, alpha=alpha),
        grid=(m // tm,),
        in_specs=[row, row, vec, vec],
        out_specs=[row, row],
        out_shape=[jax.ShapeDtypeStruct((m, d), F32), jax.ShapeDtypeStruct((m, d), BF16)],
        compiler_params=_params(("parallel",)),
        name="add_ln",
    )(h, y, g.reshape(1, d), b.reshape(1, d))


def _level_masks(c):
    t = lax.broadcasted_iota(jnp.int32, (c, c), 0)
    s = lax.broadcasted_iota(jnp.int32, (c, c), 1)
    x = t ^ s
    masks = [t == s]
    for p in range(1, int(math.log2(c)) + 1):
        masks.append((t > s) & ((x >> (p - 1)) == 1))
    return masks


def _midpoint_rows(cum_ref, c, k, p):
    blk = 1 << p
    half = blk >> 1
    if blk >= SUBLANES:
        parts = [jnp.broadcast_to(cum_ref[pl.ds(b * blk + half - 1, 1), :], (blk, k))
                 for b in range(c // blk)]
        return parts[0] if len(parts) == 1 else jnp.concatenate(parts, axis=0)
    sub = lax.broadcasted_iota(jnp.int32, (c, k), 0) & (SUBLANES - 1)
    res = None
    for j in range(SUBLANES // blk):
        rows = jnp.concatenate(
            [jnp.broadcast_to(cum_ref[pl.ds(SUBLANES * v + j * blk + half - 1, 1), :], (SUBLANES, k))
             for v in range(c // SUBLANES)], axis=0)
        res = rows if res is None else jnp.where(sub >= j * blk, rows, res)
    return res


def _split3(x):
    hi = x.astype(BF16)
    r1 = x - hi.astype(F32)
    mid = r1.astype(BF16)
    lo = (r1 - mid.astype(F32)).astype(BF16)
    return hi, mid, lo


def _gla_chunk(q, k, v, g, state_ref, cum_ref, masks, tri):
    c, kd = q.shape
    g_hi, g_mid, g_lo = _split3(g)
    cum = _dot(tri, g_hi) + _dot(tri, g_mid) + _dot(tri, g_lo)
    cum_ref[...] = cum
    row = lax.broadcasted_iota(jnp.int32, (c, kd), 0)
    qb = q.astype(BF16)
    kb = k.astype(BF16)
    vb = v.astype(BF16)
    scores = jnp.where(masks[0], _dot_nt(qb, kb), 0.0)
    for p in range(1, len(masks)):
        mid = _midpoint_rows(cum_ref, c, kd, p)
        upper = ((row >> (p - 1)) & 1) == 1
        z = (jnp.where(upper, q, k) * jnp.exp(-jnp.abs(cum - mid))).astype(BF16)
        scores = jnp.where(masks[p], _dot_nt(z, z), scores)
    state = state_ref[...]
    out = _dot(scores.astype(BF16), vb)
    out = out + _dot_nt((q * jnp.exp(cum)).astype(BF16), state.astype(BF16))
    last = cum_ref[pl.ds(c - 1, 1), :]
    k_dec = (k * jnp.exp(last - cum)).astype(BF16)
    state_ref[...] = state * jnp.exp(last) + _dot_tn(vb, k_dec)
    return out


def _tri_ones(c):
    t = lax.broadcasted_iota(jnp.int32, (c, c), 0)
    s = lax.broadcasted_iota(jnp.int32, (c, c), 1)
    return jnp.where(t >= s, 1.0, 0.0).astype(BF16)


def _hgrn_kernel(q_ref, f_ref, i_ref, g_ref, lb_ref, nw_ref, o_ref, state_ref, cum_ref, *, layer, chunk):
    @pl.when(pl.program_id(2) == 0)
    def _():
        state_ref[...] = jnp.zeros_like(state_ref)

    lb_all = lb_ref[...]
    e = jnp.exp(lb_all - jnp.max(lb_all, axis=0, keepdims=True))
    soft = e / jnp.sum(e, axis=0, keepdims=True)
    lower = jnp.zeros((1, LANES), F32)
    for l in range(1, layer + 1):
        lower = lower + soft[l:l + 1, :]
    masks = _level_masks(chunk)
    tri = _tri_ones(chunk)
    kd = q_ref.shape[1]
    for c in range(q_ref.shape[0] // chunk):
        rows = pl.ds(c * chunk, chunk)
        f = lower + (1.0 - lower) * _sigmoid(f_ref[rows, :])
        q = _silu(q_ref[rows, :]) * (kd ** -0.5)
        o = _gla_chunk(q, 1.0 - f, i_ref[rows, :], jnp.log(f), state_ref, cum_ref, masks, tri)
        o = o * lax.rsqrt(jnp.mean(o * o, axis=-1, keepdims=True) + RMS_EPS) * nw_ref[...]
        o_ref[rows, :] = (o * _silu(g_ref[rows, :])).astype(o_ref.dtype)


def _hgrn(proj, hgrn_lb, norm_w, *, layer, batch, seq):
    nt = seq // MIX_TT

    def col(cb):
        return pl.BlockSpec((MIX_TT, LANES), lambda b, h, t: (b * nt + t, cb + h))

    return pl.pallas_call(
        functools.partial(_hgrn_kernel, layer=layer, chunk=MIX_CHUNK),
        grid=(batch, HGRN_HEADS, nt),
        in_specs=[col(CB_A_Q), col(CB_A_F), col(CB_A_I), col(CB_A_G),
                  pl.BlockSpec((hgrn_lb.shape[0], LANES), lambda b, h, t: (0, h)),
                  pl.BlockSpec((1, LANES), lambda b, h, t: (0, 0))],
        out_specs=pl.BlockSpec((MIX_TT, LANES), lambda b, h, t: (b * nt + t, h)),
        out_shape=jax.ShapeDtypeStruct((batch * seq, GROUP_WIDTH), BF16),
        scratch_shapes=[pltpu.VMEM((LANES, LANES), F32), pltpu.VMEM((MIX_CHUNK, LANES), F32)],
        compiler_params=_params(("parallel", "parallel", "arbitrary")),
        name="hgrn",
    )(proj, proj, proj, proj, hgrn_lb, norm_w.reshape(1, LANES))


def _gla_kernel(q_ref, k_ref, v_ref, g_ref, r_ref, wg_ref, bg_ref, nw_ref, o_ref, state_ref, cum_ref, *, chunk):
    @pl.when(pl.program_id(2) == 0)
    def _():
        state_ref[...] = jnp.zeros_like(state_ref)

    masks = _level_masks(chunk)
    tri = _tri_ones(chunk)
    kd = q_ref.shape[1]
    for c in range(q_ref.shape[0] // chunk):
        rows = pl.ds(c * chunk, chunk)
        pre = _dot(r_ref[rows, :], wg_ref[...]) + bg_ref[...]
        log_alpha = (jnp.minimum(pre, 0.0) - jnp.log1p(jnp.exp(-jnp.abs(pre)))) * (1.0 / GLA_NORMALIZER)
        o = _gla_chunk(q_ref[rows, :] * (kd ** -0.5), k_ref[rows, :], v_ref[rows, :], log_alpha,
                       state_ref, cum_ref, masks, tri)
        o = o * lax.rsqrt(jnp.mean(o * o, axis=-1, keepdims=True) + RMS_EPS) * nw_ref[...]
        o_ref[rows, :] = (o * _silu(g_ref[rows, :])).astype(o_ref.dtype)


def _gla(proj, w_gate_padded, b_gate, norm_w, *, batch, seq):
    nt = seq // MIX_TT
    vb = GLA_VAL // LANES

    return pl.pallas_call(
        functools.partial(_gla_kernel, chunk=MIX_CHUNK),
        grid=(batch, GLA_HEADS, nt),
        in_specs=[pl.BlockSpec((MIX_TT, GLA_KEY), lambda b, h, t: (b * nt + t, CB_C_Q + h)),
                  pl.BlockSpec((MIX_TT, GLA_KEY), lambda b, h, t: (b * nt + t, CB_C_K + h)),
                  pl.BlockSpec((MIX_TT, GLA_VAL), lambda b, h, t: (b * nt + t, CB_C_V // vb + h)),
                  pl.BlockSpec((MIX_TT, GLA_VAL), lambda b, h, t: (b * nt + t, CB_C_G // vb + h)),
                  pl.BlockSpec((MIX_TT, LANES), lambda b, h, t: (b * nt + t, CB_C_R)),
                  pl.BlockSpec((LANES, GLA_KEY), lambda b, h, t: (0, h)),
                  pl.BlockSpec((1, GLA_KEY), lambda b, h, t: (0, h)),
                  pl.BlockSpec((1, GLA_VAL), lambda b, h, t: (0, 0))],
        out_specs=pl.BlockSpec((MIX_TT, GLA_VAL), lambda b, h, t: (b * nt + t, h)),
        out_shape=jax.ShapeDtypeStruct((batch * seq, GROUP_WIDTH), BF16),
        scratch_shapes=[pltpu.VMEM((GLA_VAL, GLA_KEY), F32), pltpu.VMEM((MIX_CHUNK, GLA_KEY), F32)],
        compiler_params=_params(("parallel", "parallel", "arbitrary")),
        name="gla",
    )(proj, proj, proj, proj, proj, w_gate_padded, b_gate.reshape(1, -1), norm_w.reshape(1, GLA_VAL))


def _ret_kernel(q_ref, k_ref, v_ref, g_ref, cos_ref, sin_ref, dmat_ref, qdec_ref, kdec_ref, cdec_ref, nw_ref,
                o_ref, state_ref, *, chunk):
    @pl.when(pl.program_id(2) == 0)
    def _():
        state_ref[...] = jnp.zeros_like(state_ref)

    kd = q_ref.shape[1]
    half = kd // 2
    dmat = dmat_ref[...]
    qdec = qdec_ref[...]
    kdec = kdec_ref[...]
    cdec = cdec_ref[...]
    for c in range(q_ref.shape[0] // chunk):
        rows = pl.ds(c * chunk, chunk)
        cos = cos_ref[rows, :]
        sin = sin_ref[rows, :]
        q = q_ref[rows, :]
        k = k_ref[rows, :]
        q = q * cos + pltpu.roll(q, half, 1) * sin
        k = (k * cos + pltpu.roll(k, half, 1) * sin) * (kd ** -0.5)
        vb = v_ref[rows, :].astype(BF16)
        state = state_ref[...]
        scores = _dot_nt(q.astype(BF16), k.astype(BF16)) * dmat
        o = _dot(scores.astype(BF16), vb) + _dot_nt((q * qdec).astype(BF16), state.astype(BF16))
        state_ref[...] = state * cdec + _dot_tn(vb, (k * kdec).astype(BF16))
        mu = jnp.mean(o, axis=-1, keepdims=True)
        oc = o - mu
        var = jnp.mean(oc * oc, axis=-1, keepdims=True)
        o = oc * lax.rsqrt(var + LN_EPS) * nw_ref[...]
        o_ref[rows, :] = (o * _silu(g_ref[rows, :])).astype(o_ref.dtype)


def _retention_tables(seq, chunk):
    half = LANES // 2
    inv_freq = ROPE_BASE ** (-jnp.arange(half, dtype=F32) / half)
    ang = jnp.arange(seq, dtype=F32)[:, None] * inv_freq[None, :]
    cos, sin = jnp.cos(ang), jnp.sin(ang)
    cos2 = jnp.concatenate([cos, cos], axis=1)
    sin2 = jnp.concatenate([-sin, sin], axis=1)
    log_gamma = jnp.log1p(-(2.0 ** (-5.0 - jnp.arange(RET_HEADS, dtype=F32))))
    pos = jnp.arange(chunk, dtype=F32)
    rel = pos[:, None] - pos[None, :]
    dmat = jnp.where(rel >= 0, jnp.exp(log_gamma[:, None, None] * jnp.maximum(rel, 0.0)), 0.0)
    qdec = jnp.exp(log_gamma[:, None] * (pos + 1.0)[None, :])
    kdec = jnp.exp(log_gamma[:, None] * (chunk - 1.0 - pos)[None, :])
    cdec = jnp.exp(log_gamma * chunk)
    bcast = lambda a: jnp.broadcast_to(a[..., None], a.shape + (LANES,))
    return cos2, sin2, dmat, bcast(qdec), bcast(kdec), bcast(cdec[:, None])


def _retention(proj, tables, norm_w, *, batch, seq):
    nt = seq // MIX_TT
    cos2, sin2, dmat, qdec, kdec, cdec = tables
    chunk = dmat.shape[1]

    def col(cb):
        return pl.BlockSpec((MIX_TT, LANES), lambda b, h, t: (b * nt + t, cb + h))

    rot = pl.BlockSpec((MIX_TT, LANES), lambda b, h, t: (t, 0))
    return pl.pallas_call(
        functools.partial(_ret_kernel, chunk=chunk),
        grid=(batch, RET_HEADS, nt),
        in_specs=[col(CB_D_Q), col(CB_D_K), col(CB_D_V), col(CB_D_G), rot, rot,
                  pl.BlockSpec((None, chunk, chunk), lambda b, h, t: (h, 0, 0)),
                  pl.BlockSpec((None, chunk, LANES), lambda b, h, t: (h, 0, 0)),
                  pl.BlockSpec((None, chunk, LANES), lambda b, h, t: (h, 0, 0)),
                  pl.BlockSpec((None, 1, LANES), lambda b, h, t: (h, 0, 0)),
                  pl.BlockSpec((1, LANES), lambda b, h, t: (0, 0))],
        out_specs=pl.BlockSpec((MIX_TT, LANES), lambda b, h, t: (b * nt + t, h)),
        out_shape=jax.ShapeDtypeStruct((batch * seq, GROUP_WIDTH), BF16),
        scratch_shapes=[pltpu.VMEM((LANES, LANES), F32)],
        compiler_params=_params(("parallel", "parallel", "arbitrary")),
        name="retention",
    )(proj, proj, proj, proj, cos2, sin2, dmat, qdec, kdec, cdec, norm_w.reshape(1, LANES))


def _lru_kernel(x_ref, y_ref, cw_ref, cb_ref, wa_ref, ba_ref, wx_ref, bx_ref, lam_ref, o_ref,
                xbuf_ref, a_ref, u_ref, carry_ref, *, scan_chunk):
    tt = x_ref.shape[0]

    @pl.when(pl.program_id(2) == 0)
    def _():
        xbuf_ref[pl.ds(0, SUBLANES), :] = jnp.zeros((SUBLANES, LANES), F32)
        carry_ref[...] = jnp.zeros_like(carry_ref)

    xbuf_ref[pl.ds(SUBLANES, tt), :] = x_ref[...]
    xc = cb_ref[...]
    for j in range(CONV_WIDTH):
        xc = xc + cw_ref[pl.ds(j, 1), :] * xbuf_ref[pl.ds(SUBLANES - (CONV_WIDTH - 1) + j, tt), :]
    xbuf_ref[pl.ds(0, SUBLANES), :] = x_ref[pl.ds(tt - SUBLANES, SUBLANES), :]

    xcb = xc.astype(BF16)
    r = _sigmoid(_dot(xcb, wa_ref[...].astype(BF16)) + ba_ref[...])
    i = _sigmoid(_dot(xcb, wx_ref[...].astype(BF16)) + bx_ref[...])
    log_a = (-LRU_C) * r * _softplus(-lam_ref[...])
    a_ref[...] = jnp.exp(log_a)
    th = jnp.tanh(log_a)
    u_ref[...] = jnp.sqrt(-2.0 * th / (1.0 - th)) * (i * xc)

    row = lax.broadcasted_iota(jnp.int32, (scan_chunk, LANES), 0)
    for c in range(tt // scan_chunk):
        rows = pl.ds(c * scan_chunk, scan_chunk)
        a = a_ref[rows, :]
        u = u_ref[rows, :]
        shift = 1
        while shift < scan_chunk:
            valid = row >= shift
            u = jnp.where(valid, a * pltpu.roll(u, shift, 0) + u, u)
            a = jnp.where(valid, a * pltpu.roll(a, shift, 0), a)
            shift *= 2
        h = a * carry_ref[...] + u
        carry_ref[...] = h[scan_chunk - 1:scan_chunk, :]
        o_ref[rows, :] = (h * jax.nn.gelu(y_ref[rows, :])).astype(o_ref.dtype)


def _rg_lru(proj, conv_w, conv_b, wa, ba, wx, bx, lam, *, batch, seq):
    nt = seq // MIX_TT
    vec = pl.BlockSpec((1, LANES), lambda b, n, t: (0, n))
    mat = pl.BlockSpec((None, LANES, LANES), lambda b, n, t: (n, 0, 0))
    return pl.pallas_call(
        functools.partial(_lru_kernel, scan_chunk=SCAN_CHUNK),
        grid=(batch, LRU_BLOCKS, nt),
        in_specs=[pl.BlockSpec((MIX_TT, LANES), lambda b, n, t: (b * nt + t, CB_B_X + n)),
                  pl.BlockSpec((MIX_TT, LANES), lambda b, n, t: (b * nt + t, CB_B_Y + n)),
                  pl.BlockSpec((CONV_WIDTH, LANES), lambda b, n, t: (0, n)),
                  vec, mat, vec, mat, vec, vec],
        out_specs=pl.BlockSpec((MIX_TT, LANES), lambda b, n, t: (b * nt + t, n)),
        out_shape=jax.ShapeDtypeStruct((batch * seq, GROUP_WIDTH), BF16),
        scratch_shapes=[pltpu.VMEM((MIX_TT + SUBLANES, LANES), F32),
                        pltpu.VMEM((MIX_TT, LANES), F32),
                        pltpu.VMEM((MIX_TT, LANES), F32),
                        pltpu.VMEM((1, LANES), F32)],
        compiler_params=_params(("parallel", "parallel", "arbitrary")),
        name="rg_lru",
    )(proj, proj, conv_w, conv_b.reshape(1, -1), wa, ba.reshape(1, -1), wx, bx.reshape(1, -1),
      lam.reshape(1, -1))


def kernel(x, w_in, w_out, hgrn_lb, hgrn_norm, conv_w, conv_b, lru_wa, lru_ba, lru_wx, lru_bx, lru_lambda,
           gla_w_gate, gla_b_gate, gla_norm, ret_norm, ln1_g, ln1_b, ln2_g, ln2_b, ffn_w_gate, ffn_w_up,
           ffn_w_down, router_w, exp_w_gate, exp_w_up, exp_w_down):
    batch, seq, d = x.shape
    depth = w_in.shape[0]
    m = batch * seq
    alpha = (2 * depth) ** 0.25
    tables = _retention_tables(seq, MIX_CHUNK)

    h = x.reshape(m, d)
    hb = h.astype(BF16)
    for layer in range(depth):
        wl = w_in[layer]
        pad = jnp.zeros((d, D_IN_PADDED - wl.shape[1]), F32)
        w_in_l = jnp.concatenate(
            [wl[:, :GATE_COL0], wl[:, GATE_COL0 + GLA_RANK:], wl[:, GATE_COL0:GATE_COL0 + GLA_RANK], pad],
            axis=1).astype(BF16)
        proj = _matmul(hb, w_in_l, tm=1024, tn=512, out_dtype=F32, name="in_proj")

        w_gate_p = jnp.concatenate(
            [gla_w_gate[layer], jnp.zeros((LANES - GLA_RANK, gla_w_gate.shape[2]), F32)], axis=0)
        o_a = _hgrn(proj, hgrn_lb, hgrn_norm[layer], layer=layer, batch=batch, seq=seq)
        o_b = _rg_lru(proj, conv_w[layer], conv_b[layer], lru_wa[layer], lru_ba[layer], lru_wx[layer],
                      lru_bx[layer], lru_lambda[layer], batch=batch, seq=seq)
        o_c = _gla(proj, w_gate_p, gla_b_gate[layer], gla_norm[layer], batch=batch, seq=seq)
        o_d = _retention(proj, tables, ret_norm[layer], batch=batch, seq=seq)
        mixed = jnp.concatenate([o_a, o_b, o_c, o_d], axis=1)
        mix = _matmul(mixed, w_out[layer].astype(BF16), tm=1024, tn=1024, out_dtype=F32, name="out_proj")
        h, hb = _add_ln(h, mix, ln1_g[layer], ln1_b[layer], alpha=alpha, tm=256)

        j = layer // 2
        if layer % 2 == 0:
            hid = _swiglu_up(hb, ffn_w_gate[j].astype(BF16), ffn_w_up[j].astype(BF16), tm=1024, tn=512)
            ff = _matmul(hid, ffn_w_down[j].astype(BF16), tm=512, tn=1024, tk=4096, out_dtype=F32,
                         name="ffn_down")
        else:
            rw = jnp.concatenate([router_w[j], jnp.zeros((d, LANES - N_EXPERTS), F32)], axis=1)
            gates = _router(h, rw, tm=256)
            hid = _moe_up(hb, exp_w_gate[j].astype(BF16), exp_w_up[j].astype(BF16), gates, tm=1024, tn=512)
            wd = exp_w_down[j].reshape(-1, d).astype(BF16)
            ff = _matmul(hid, wd, tm=512, tn=1024, tk=4096, out_dtype=F32, name="moe_down")
        h, hb = _add_ln(h, ff, ln2_g[layer], ln2_b[layer], alpha=alpha, tm=256)
    return h.reshape(batch, seq, d)
```

```python
import functools
import math

import jax
import jax.numpy as jnp
import numpy as np
from jax import lax
from jax.experimental import pallas as pl
from jax.experimental.pallas import tpu as pltpu

F32 = jnp.float32
BF16 = jnp.bfloat16

LANES = 128
SUBLANES = 8
VMEM_LIMIT = 56 * 1024 * 1024

D_MODEL = 4096
N_GROUPS = 4
GROUP_WIDTH = D_MODEL // N_GROUPS
HGRN_HEADS = 8
LRU_BLOCKS = 8
CONV_WIDTH = 4
LRU_C = 8.0
GLA_HEADS = 4
GLA_KEY = 128
GLA_VAL = 256
GLA_RANK = 16
GLA_NORMALIZER = 16.0
RET_HEADS = 8
ROPE_BASE = 10000.0
N_EXPERTS = 8
LN_EPS = 1e-5
RMS_EPS = 1e-6

CB_A_Q, CB_A_F, CB_A_I, CB_A_G = 0, 8, 16, 24
CB_B_X, CB_B_Y = 32, 40
CB_C_Q, CB_C_K, CB_C_V, CB_C_G = 48, 52, 56, 64
CB_D_Q, CB_D_K, CB_D_V, CB_D_G = 0, 8, 16, 24
ABC_COLS = 9216
D_COL0 = ABC_COLS + GLA_RANK

MIX_CHUNK = 128
MIX_TT = 1024
SCAN_CHUNK = 64
MOE_TR = 256
TOP_K = 2


def _params(semantics):
    return pltpu.CompilerParams(dimension_semantics=semantics, vmem_limit_bytes=VMEM_LIMIT)


def _sigmoid(x):
    return 1.0 / (1.0 + jnp.exp(-x))


def _silu(x):
    return x * _sigmoid(x)


def _softplus(x):
    return jnp.maximum(x, 0.0) + jnp.log1p(jnp.exp(-jnp.abs(x)))


def _dot(a, b):
    return jnp.dot(a, b, preferred_element_type=F32)


def _dot_nt(a, b):
    return lax.dot_general(a, b, (((1,), (1,)), ((), ())), preferred_element_type=F32)


def _dot_tn(a, b):
    return lax.dot_general(a, b, (((0,), (0,)), ((), ())), preferred_element_type=F32)


def _matmul_kernel(a_ref, b_ref, o_ref, *scratch, nk):
    if nk == 1:
        o_ref[...] = _dot(a_ref[...], b_ref[...]).astype(o_ref.dtype)
        return
    acc_ref, = scratch
    k = pl.program_id(2)

    @pl.when(k == 0)
    def _():
        acc_ref[...] = jnp.zeros_like(acc_ref)

    acc_ref[...] += _dot(a_ref[...], b_ref[...])

    @pl.when(k == nk - 1)
    def _():
        o_ref[...] = acc_ref[...].astype(o_ref.dtype)


def _matmul(a, b, *, tm, tn, tk=None, out_dtype, name):
    m, kdim = a.shape
    _, n = b.shape
    tk = kdim if tk is None else tk
    assert m % tm == 0 and n % tn == 0 and kdim % tk == 0
    nk = kdim // tk
    return pl.pallas_call(
        functools.partial(_matmul_kernel, nk=nk),
        grid=(m // tm, n // tn, nk),
        in_specs=[pl.BlockSpec((tm, tk), lambda i, j, k: (i, k)),
                  pl.BlockSpec((tk, tn), lambda i, j, k: (k, j))],
        out_specs=pl.BlockSpec((tm, tn), lambda i, j, k: (i, j)),
        out_shape=jax.ShapeDtypeStruct((m, n), out_dtype),
        scratch_shapes=[pltpu.VMEM((tm, tn), F32)] if nk > 1 else [],
        compiler_params=_params(("parallel", "parallel", "arbitrary")),
        name=name,
    )(a, b)


def _swiglu_up_kernel(x_ref, wg_ref, wu_ref, o_ref, wgb_ref, wub_ref):
    @pl.when(pl.program_id(1) == 0)
    def _():
        wgb_ref[...] = wg_ref[...].astype(BF16)
        wub_ref[...] = wu_ref[...].astype(BF16)

    x = x_ref[...]
    g = _dot(x, wgb_ref[...])
    u = _dot(x, wub_ref[...])
    o_ref[...] = (_silu(g) * u).astype(o_ref.dtype)


def _swiglu_up(x, wg, wu, lead, *, tm, tn):
    m, d = x.shape
    f = wg.shape[2]
    wspec = pl.BlockSpec((None, d, tn), lambda j, i: (lead, 0, j))
    return pl.pallas_call(
        _swiglu_up_kernel,
        grid=(f // tn, m // tm),
        in_specs=[pl.BlockSpec((tm, d), lambda j, i: (i, 0)), wspec, wspec],
        out_specs=pl.BlockSpec((tm, tn), lambda j, i: (i, j)),
        out_shape=jax.ShapeDtypeStruct((m, f), BF16),
        scratch_shapes=[pltpu.VMEM((d, tn), BF16), pltpu.VMEM((d, tn), BF16)],
        compiler_params=_params(("parallel", "arbitrary")),
        name="swiglu_up",
    )(x, wg, wu)


def _wres_kernel(*refs, k_sizes):
    n_a = len(k_sizes)
    a_refs = refs[:n_a]
    b_ref, o_ref, wb_ref = refs[n_a:]

    @pl.when(pl.program_id(1) == 0)
    def _():
        wb_ref[...] = b_ref[...].astype(BF16)

    acc = None
    off = 0
    for a_ref, kg in zip(a_refs, k_sizes):
        part = _dot(a_ref[...], wb_ref[pl.ds(off, kg), :])
        acc = part if acc is None else acc + part
        off += kg
    o_ref[...] = acc.astype(o_ref.dtype)


def _matmul_wres(a_list, b, lead, *, n_cols, tm, tn, out_dtype, name):
    m = a_list[0].shape[0]
    k_sizes = tuple(a.shape[1] for a in a_list)
    kdim = sum(k_sizes)
    if lead is None:
        bspec = pl.BlockSpec((kdim, tn), lambda j, i: (0, j))
    else:
        bspec = pl.BlockSpec((None, kdim, tn), lambda j, i: (lead, 0, j))
    return pl.pallas_call(
        functools.partial(_wres_kernel, k_sizes=k_sizes),
        grid=(n_cols // tn, m // tm),
        in_specs=[pl.BlockSpec((tm, kg), lambda j, i: (i, 0)) for kg in k_sizes] + [bspec],
        out_specs=pl.BlockSpec((tm, tn), lambda j, i: (i, j)),
        out_shape=jax.ShapeDtypeStruct((m, n_cols), out_dtype),
        scratch_shapes=[pltpu.VMEM((kdim, tn), BF16)],
        compiler_params=_params(("parallel", "arbitrary")),
        name=name,
    )(*a_list, b)


def _router_kernel(h_ref, w_ref, ei_ref, wt_ref, rk_ref, cnt_ref, carry_ref):
    @pl.when(pl.program_id(0) == 0)
    def _():
        carry_ref[...] = jnp.zeros_like(carry_ref)

    logits = jnp.dot(h_ref[...], w_ref[...], preferred_element_type=F32,
                     precision=lax.Precision.HIGHEST)
    tm = logits.shape[0]
    lane = lax.broadcasted_iota(jnp.int32, logits.shape, 1)
    lanef = lane.astype(F32)
    neg = jnp.float32(-jnp.inf)
    logits = jnp.where(lane < N_EXPERTS, logits, neg)
    m1 = jnp.max(logits, axis=1, keepdims=True)
    i1 = jnp.min(jnp.where(logits == m1, lanef, float(LANES)), axis=1, keepdims=True)
    rest = jnp.where(lanef == i1, neg, logits)
    m2 = jnp.max(rest, axis=1, keepdims=True)
    i2 = jnp.min(jnp.where(rest == m2, lanef, float(LANES)), axis=1, keepdims=True)
    e2 = jnp.exp(m2 - m1)
    w1 = 1.0 / (1.0 + e2)
    w2 = e2 / (1.0 + e2)

    sel1 = lanef == i1
    sel2 = lanef == i2
    onehot = jnp.where(sel1 | sel2, 1.0, 0.0)
    t = lax.broadcasted_iota(jnp.int32, (tm, tm), 0)
    s = lax.broadcasted_iota(jnp.int32, (tm, tm), 1)
    before = jnp.where(t > s, 1.0, 0.0).astype(BF16)
    seen = _dot(before, onehot.astype(BF16)) + carry_ref[...]
    r1 = jnp.sum(jnp.where(sel1, seen, 0.0), axis=1, keepdims=True)
    r2 = jnp.sum(jnp.where(sel2, seen, 0.0), axis=1, keepdims=True)
    total = carry_ref[...] + jnp.sum(onehot, axis=0, keepdims=True)
    carry_ref[...] = total

    ei_ref[...] = jnp.where(lane == 0, i1, jnp.where(lane == 1, i2, 0.0)).astype(jnp.int32)
    wt_ref[...] = jnp.where(lane == 0, w1, jnp.where(lane == 1, w2, 0.0))
    rk_ref[...] = jnp.where(lane == 0, r1, jnp.where(lane == 1, r2, 0.0)).astype(jnp.int32)
    cnt_ref[...] = jnp.broadcast_to(total, cnt_ref.shape).astype(jnp.int32)


def _router(h, router_w_padded, *, tm):
    m, d = h.shape
    tok = pl.BlockSpec((tm, LANES), lambda i: (i, 0))
    return pl.pallas_call(
        _router_kernel,
        grid=(m // tm,),
        in_specs=[pl.BlockSpec((tm, d), lambda i: (i, 0)),
                  pl.BlockSpec((d, LANES), lambda i: (0, 0))],
        out_specs=[tok, tok, tok, pl.BlockSpec((SUBLANES, LANES), lambda i: (0, 0))],
        out_shape=[jax.ShapeDtypeStruct((m, LANES), jnp.int32), jax.ShapeDtypeStruct((m, LANES), F32),
                   jax.ShapeDtypeStruct((m, LANES), jnp.int32),
                   jax.ShapeDtypeStruct((SUBLANES, LANES), jnp.int32)],
        scratch_shapes=[pltpu.VMEM((1, LANES), F32)],
        compiler_params=_params(("arbitrary",)),
        name="router",
    )(h, router_w_padded)


def _dispatch_kernel(d1_ref, d2_ref, x_ref, buf_in_ref, o_ref, sem, *, td):
    del buf_in_ref
    base = pl.program_id(0) * td

    def row_copy(src_row, dst_row):
        return pltpu.make_async_copy(x_ref.at[pl.ds(src_row, 1), :], o_ref.at[pl.ds(dst_row, 1), :], sem)

    def issue(t, carry):
        row_copy(base + t, d1_ref[base + t]).start()
        row_copy(base + t, d2_ref[base + t]).start()
        return carry

    lax.fori_loop(0, td, issue, 0, unroll=8)

    def drain(t, carry):
        row_copy(0, 0).wait()
        row_copy(0, 0).wait()
        return carry

    lax.fori_loop(0, td, drain, 0, unroll=8)


def _dispatch(xp, dst1, dst2, buf, *, td):
    m = xp.shape[0]
    return pl.pallas_call(
        functools.partial(_dispatch_kernel, td=td),
        grid_spec=pltpu.PrefetchScalarGridSpec(
            num_scalar_prefetch=2, grid=(m // td,),
            in_specs=[pl.BlockSpec(memory_space=pl.ANY), pl.BlockSpec(memory_space=pl.ANY)],
            out_specs=pl.BlockSpec(memory_space=pl.ANY),
            scratch_shapes=[pltpu.SemaphoreType.DMA(())]),
        out_shape=jax.ShapeDtypeStruct(buf.shape, buf.dtype),
        input_output_aliases={3: 0},
        compiler_params=_params(("arbitrary",)),
        name="moe_dispatch",
    )(dst1, dst2, xp, buf)


def _unpack_halves(xp):
    lo = lax.bitcast_convert_type(xp << jnp.uint32(16), F32).astype(BF16)
    hi = lax.bitcast_convert_type(xp & jnp.uint32(0xFFFF0000), F32).astype(BF16)
    return lo, hi


def _moe_up_kernel(te_ref, tf_ref, tv_ref, x_ref, wg_ref, wu_ref, o_ref, wgb_ref, wub_ref):
    del te_ref
    i = pl.program_id(1)

    @pl.when(tf_ref[i] == 1)
    def _():
        wgb_ref[...] = wg_ref[...].astype(BF16)
        wub_ref[...] = wu_ref[...].astype(BF16)

    @pl.when(tv_ref[i] == 1)
    def _():
        half = x_ref.shape[1]
        lo, hi = _unpack_halves(x_ref[...])
        g = _dot(lo, wgb_ref[pl.ds(0, half), :]) + _dot(hi, wgb_ref[pl.ds(half, half), :])
        u = _dot(lo, wub_ref[pl.ds(0, half), :]) + _dot(hi, wub_ref[pl.ds(half, half), :])
        o_ref[...] = (_silu(g) * u).astype(o_ref.dtype)

    @pl.when(tv_ref[i] == 0)
    def _():
        o_ref[...] = jnp.zeros_like(o_ref)


def _moe_up(xs, wg, wu, lead, te, tf, tv, *, tn):
    rows, half = xs.shape
    d = 2 * half
    f = wg.shape[3]
    wspec = pl.BlockSpec((None, None, d, tn), lambda j, i, te, tf, tv: (lead, te[i], 0, j))
    return pl.pallas_call(
        _moe_up_kernel,
        grid_spec=pltpu.PrefetchScalarGridSpec(
            num_scalar_prefetch=3, grid=(f // tn, rows // MOE_TR),
            in_specs=[pl.BlockSpec((MOE_TR, half), lambda j, i, te, tf, tv: (i, 0)), wspec, wspec],
            out_specs=pl.BlockSpec((MOE_TR, tn), lambda j, i, te, tf, tv: (i, j)),
            scratch_shapes=[pltpu.VMEM((d, tn), BF16), pltpu.VMEM((d, tn), BF16)]),
        out_shape=jax.ShapeDtypeStruct((rows, f), BF16),
        compiler_params=_params(("parallel", "arbitrary")),
        name="moe_up",
    )(te, tf, tv, xs, wg, wu)


def _moe_down_kernel(te_ref, tf_ref, tv_ref, x_ref, w_ref, o_ref, wb_ref):
    del te_ref
    i = pl.program_id(1)

    @pl.when(tf_ref[i] == 1)
    def _():
        wb_ref[...] = w_ref[...].astype(BF16)

    @pl.when(tv_ref[i] == 1)
    def _():
        o_ref[...] = _dot(x_ref[...], wb_ref[...])

    @pl.when(tv_ref[i] == 0)
    def _():
        o_ref[...] = jnp.zeros_like(o_ref)


def _moe_down(hid, wd, lead, te, tf, tv, *, tn):
    rows, f = hid.shape
    d = wd.shape[3]
    return pl.pallas_call(
        _moe_down_kernel,
        grid_spec=pltpu.PrefetchScalarGridSpec(
            num_scalar_prefetch=3, grid=(d // tn, rows // MOE_TR),
            in_specs=[pl.BlockSpec((MOE_TR, f), lambda j, i, te, tf, tv: (i, 0)),
                      pl.BlockSpec((None, None, f, tn), lambda j, i, te, tf, tv: (lead, te[i], 0, j))],
            out_specs=pl.BlockSpec((MOE_TR, tn), lambda j, i, te, tf, tv: (i, j)),
            scratch_shapes=[pltpu.VMEM((f, tn), BF16)]),
        out_shape=jax.ShapeDtypeStruct((rows, d), F32),
        compiler_params=_params(("parallel", "arbitrary")),
        name="moe_down",
    )(te, tf, tv, hid, wd)


def _combine_ln_kernel(d1_ref, d2_ref, y_ref, h_ref, wt_ref, g_ref, b_ref, o_ref, ob_ref,
                       buf1, buf2, sem, *, alpha, tc):
    base = pl.program_id(0) * tc

    def row_copy(src_row, buf, t):
        return pltpu.make_async_copy(y_ref.at[pl.ds(src_row, 1), :], buf.at[pl.ds(t, 1), :], sem)

    def issue(t, carry):
        row_copy(d1_ref[base + t], buf1, t).start()
        row_copy(d2_ref[base + t], buf2, t).start()
        return carry

    lax.fori_loop(0, tc, issue, 0, unroll=8)

    def drain(t, carry):
        row_copy(0, buf1, 0).wait()
        row_copy(0, buf2, 0).wait()
        return carry

    lax.fori_loop(0, tc, drain, 0, unroll=8)

    wt = wt_ref[...]
    x = alpha * h_ref[...] + wt[:, 0:1] * buf1[...] + wt[:, 1:2] * buf2[...]
    mu = jnp.mean(x, axis=-1, keepdims=True)
    xc = x - mu
    var = jnp.mean(xc * xc, axis=-1, keepdims=True)
    out = xc * lax.rsqrt(var + LN_EPS) * g_ref[...] + b_ref[...]
    o_ref[...] = out
    ob_ref[...] = out.astype(BF16)


def _combine_ln(y, dst1, dst2, h, wt, g, b, *, alpha, tc):
    m, d = h.shape
    row = pl.BlockSpec((tc, d), lambda i, d1, d2: (i, 0))
    vec = pl.BlockSpec((1, d), lambda i, d1, d2: (0, 0))
    return pl.pallas_call(
        functools.partial(_combine_ln_kernel, alpha=alpha, tc=tc),
        grid_spec=pltpu.PrefetchScalarGridSpec(
            num_scalar_prefetch=2, grid=(m // tc,),
            in_specs=[pl.BlockSpec(memory_space=pl.ANY), row,
                      pl.BlockSpec((tc, LANES), lambda i, d1, d2: (i, 0)), vec, vec],
            out_specs=[row, row],
            scratch_shapes=[pltpu.VMEM((tc, d), F32), pltpu.VMEM((tc, d), F32),
                            pltpu.SemaphoreType.DMA(())]),
        out_shape=[jax.ShapeDtypeStruct((m, d), F32), jax.ShapeDtypeStruct((m, d), BF16)],
        compiler_params=_params(("arbitrary",)),
        name="moe_combine_ln",
    )(dst1, dst2, y, h, wt, g.reshape(1, d), b.reshape(1, d))


def _add_ln_kernel(h_ref, y_ref, g_ref, b_ref, o_ref, ob_ref, *maybe_packed_ref, alpha):
    x = alpha * h_ref[...] + y_ref[...]
    mu = jnp.mean(x, axis=-1, keepdims=True)
    xc = x - mu
    var = jnp.mean(xc * xc, axis=-1, keepdims=True)
    out = xc * lax.rsqrt(var + LN_EPS) * g_ref[...] + b_ref[...]
    o_ref[...] = out
    out_b = out.astype(BF16)
    ob_ref[...] = out_b
    if maybe_packed_ref:
        half = out.shape[1] // 2
        as_u32 = lax.bitcast_convert_type(out_b.astype(F32), jnp.uint32)
        maybe_packed_ref[0][...] = (as_u32[:, :half] >> jnp.uint32(16)) | as_u32[:, half:]


def _add_ln(h, y, g, b, *, alpha, tm, packed=False):
    m, d = h.shape
    row = pl.BlockSpec((tm, d), lambda i: (i, 0))
    vec = pl.BlockSpec((1, d), lambda i: (0, 0))
    out_specs = [row, row]
    out_shape = [jax.ShapeDtypeStruct((m, d), F32), jax.ShapeDtypeStruct((m, d), BF16)]
    if packed:
        out_specs.append(pl.BlockSpec((tm, d // 2), lambda i: (i, 0)))
        out_shape.append(jax.ShapeDtypeStruct((m, d // 2), jnp.uint32))
    return pl.pallas_call(
        functools.partial(_add_ln_kernel, alpha=alpha),
        grid=(m // tm,),
        in_specs=[row, row, vec, vec],
        out_specs=out_specs,
        out_shape=out_shape,
        compiler_params=_params(("parallel",)),
        name="add_ln",
    )(h, y, g.reshape(1, d), b.reshape(1, d))


def _level_masks(c):
    t = lax.broadcasted_iota(jnp.int32, (c, c), 0)
    s = lax.broadcasted_iota(jnp.int32, (c, c), 1)
    x = t ^ s
    masks = [t == s]
    for p in range(1, int(math.log2(c)) + 1):
        masks.append((t > s) & ((x >> (p - 1)) == 1))
    return masks


def _midpoint_rows(cum_ref, c, k, p):
    blk = 1 << p
    half = blk >> 1
    if blk >= SUBLANES:
        parts = [jnp.broadcast_to(cum_ref[pl.ds(b * blk + half - 1, 1), :], (blk, k))
                 for b in range(c // blk)]
        return parts[0] if len(parts) == 1 else jnp.concatenate(parts, axis=0)
    sub = lax.broadcasted_iota(jnp.int32, (c, k), 0) & (SUBLANES - 1)
    res = None
    for j in range(SUBLANES // blk):
        rows = jnp.concatenate(
            [jnp.broadcast_to(cum_ref[pl.ds(SUBLANES * v + j * blk + half - 1, 1), :], (SUBLANES, k))
             for v in range(c // SUBLANES)], axis=0)
        res = rows if res is None else jnp.where(sub >= j * blk, rows, res)
    return res


def _split3(x):
    hi = x.astype(BF16)
    r1 = x - hi.astype(F32)
    mid = r1.astype(BF16)
    lo = (r1 - mid.astype(F32)).astype(BF16)
    return hi, mid, lo


def _gla_chunk(q, k, v, g, state_ref, cum_ref, masks, tri):
    c, kd = q.shape
    g_hi, g_mid, g_lo = _split3(g)
    cum = _dot(tri, g_hi) + _dot(tri, g_mid) + _dot(tri, g_lo)
    cum_ref[...] = cum
    row = lax.broadcasted_iota(jnp.int32, (c, kd), 0)
    qb = q.astype(BF16)
    kb = k.astype(BF16)
    vb = v.astype(BF16)
    scores = jnp.where(masks[0], _dot_nt(qb, kb), 0.0)
    for p in range(1, len(masks)):
        mid = _midpoint_rows(cum_ref, c, kd, p)
        upper = ((row >> (p - 1)) & 1) == 1
        z = (jnp.where(upper, q, k) * jnp.exp(-jnp.abs(cum - mid))).astype(BF16)
        scores = jnp.where(masks[p], _dot_nt(z, z), scores)
    state = state_ref[...]
    out = _dot(scores.astype(BF16), vb)
    out = out + _dot_nt((q * jnp.exp(cum)).astype(BF16), state.astype(BF16))
    last = cum_ref[pl.ds(c - 1, 1), :]
    k_dec = (k * jnp.exp(last - cum)).astype(BF16)
    state_ref[...] = state * jnp.exp(last) + _dot_tn(vb, k_dec)
    return out


def _tri_ones(c):
    t = lax.broadcasted_iota(jnp.int32, (c, c), 0)
    s = lax.broadcasted_iota(jnp.int32, (c, c), 1)
    return jnp.where(t >= s, 1.0, 0.0).astype(BF16)


def _hgrn_kernel(q_ref, f_ref, i_ref, g_ref, lb_ref, nw_ref, o_ref, state_ref, cum_ref, *, layer, chunk):
    @pl.when(pl.program_id(2) == 0)
    def _():
        state_ref[...] = jnp.zeros_like(state_ref)

    lb_all = lb_ref[...]
    e = jnp.exp(lb_all - jnp.max(lb_all, axis=0, keepdims=True))
    soft = e / jnp.sum(e, axis=0, keepdims=True)
    lower = jnp.zeros((1, LANES), F32)
    for l in range(1, layer + 1):
        lower = lower + soft[l:l + 1, :]
    masks = _level_masks(chunk)
    tri = _tri_ones(chunk)
    kd = q_ref.shape[1]
    for c in range(q_ref.shape[0] // chunk):
        rows = pl.ds(c * chunk, chunk)
        f = lower + (1.0 - lower) * _sigmoid(f_ref[rows, :])
        q = _silu(q_ref[rows, :]) * (kd ** -0.5)
        o = _gla_chunk(q, 1.0 - f, i_ref[rows, :], jnp.log(f), state_ref, cum_ref, masks, tri)
        o = o * lax.rsqrt(jnp.mean(o * o, axis=-1, keepdims=True) + RMS_EPS) * nw_ref[...]
        o_ref[rows, :] = (o * _silu(g_ref[rows, :])).astype(o_ref.dtype)


def _hgrn(proj, hgrn_lb, norm_w, *, layer, batch, seq):
    nt = seq // MIX_TT

    def col(cb):
        return pl.BlockSpec((MIX_TT, LANES), lambda b, h, t: (b * nt + t, cb + h))

    return pl.pallas_call(
        functools.partial(_hgrn_kernel, layer=layer, chunk=MIX_CHUNK),
        grid=(batch, HGRN_HEADS, nt),
        in_specs=[col(CB_A_Q), col(CB_A_F), col(CB_A_I), col(CB_A_G),
                  pl.BlockSpec((hgrn_lb.shape[0], LANES), lambda b, h, t: (0, h)),
                  pl.BlockSpec((1, LANES), lambda b, h, t: (0, 0))],
        out_specs=pl.BlockSpec((MIX_TT, LANES), lambda b, h, t: (b * nt + t, h)),
        out_shape=jax.ShapeDtypeStruct((batch * seq, GROUP_WIDTH), BF16),
        scratch_shapes=[pltpu.VMEM((LANES, LANES), F32), pltpu.VMEM((MIX_CHUNK, LANES), F32)],
        compiler_params=_params(("parallel", "parallel", "arbitrary")),
        name="hgrn",
    )(proj, proj, proj, proj, hgrn_lb, norm_w.reshape(1, LANES))


def _gla_kernel(q_ref, k_ref, v_ref, g_ref, r_ref, wg_ref, bg_ref, nw_ref, o_ref, state_ref, cum_ref, *, chunk):
    @pl.when(pl.program_id(2) == 0)
    def _():
        state_ref[...] = jnp.zeros_like(state_ref)

    masks = _level_masks(chunk)
    tri = _tri_ones(chunk)
    kd = q_ref.shape[1]
    for c in range(q_ref.shape[0] // chunk):
        rows = pl.ds(c * chunk, chunk)
        pre = _dot(r_ref[rows, :], wg_ref[...]) + bg_ref[...]
        log_alpha = (jnp.minimum(pre, 0.0) - jnp.log1p(jnp.exp(-jnp.abs(pre)))) * (1.0 / GLA_NORMALIZER)
        o = _gla_chunk(q_ref[rows, :] * (kd ** -0.5), k_ref[rows, :], v_ref[rows, :], log_alpha,
                       state_ref, cum_ref, masks, tri)
        o = o * lax.rsqrt(jnp.mean(o * o, axis=-1, keepdims=True) + RMS_EPS) * nw_ref[...]
        o_ref[rows, :] = (o * _silu(g_ref[rows, :])).astype(o_ref.dtype)


def _gla(proj, proj_r, w_gate_padded, b_gate, norm_w, *, batch, seq):
    nt = seq // MIX_TT
    vb = GLA_VAL // LANES

    return pl.pallas_call(
        functools.partial(_gla_kernel, chunk=MIX_CHUNK),
        grid=(batch, GLA_HEADS, nt),
        in_specs=[pl.BlockSpec((MIX_TT, GLA_KEY), lambda b, h, t: (b * nt + t, CB_C_Q + h)),
                  pl.BlockSpec((MIX_TT, GLA_KEY), lambda b, h, t: (b * nt + t, CB_C_K + h)),
                  pl.BlockSpec((MIX_TT, GLA_VAL), lambda b, h, t: (b * nt + t, CB_C_V // vb + h)),
                  pl.BlockSpec((MIX_TT, GLA_VAL), lambda b, h, t: (b * nt + t, CB_C_G // vb + h)),
                  pl.BlockSpec((MIX_TT, LANES), lambda b, h, t: (b * nt + t, 0)),
                  pl.BlockSpec((LANES, GLA_KEY), lambda b, h, t: (0, h)),
                  pl.BlockSpec((1, GLA_KEY), lambda b, h, t: (0, h)),
                  pl.BlockSpec((1, GLA_VAL), lambda b, h, t: (0, 0))],
        out_specs=pl.BlockSpec((MIX_TT, GLA_VAL), lambda b, h, t: (b * nt + t, h)),
        out_shape=jax.ShapeDtypeStruct((batch * seq, GROUP_WIDTH), BF16),
        scratch_shapes=[pltpu.VMEM((GLA_VAL, GLA_KEY), F32), pltpu.VMEM((MIX_CHUNK, GLA_KEY), F32)],
        compiler_params=_params(("parallel", "parallel", "arbitrary")),
        name="gla",
    )(proj, proj, proj, proj, proj_r, w_gate_padded, b_gate.reshape(1, -1), norm_w.reshape(1, GLA_VAL))


def _ret_kernel(q_ref, k_ref, v_ref, g_ref, cos_ref, sin_ref, dmat_ref, qdec_ref, kdec_ref, cdec_ref, nw_ref,
                o_ref, state_ref, *, chunk):
    @pl.when(pl.program_id(2) == 0)
    def _():
        state_ref[...] = jnp.zeros_like(state_ref)

    kd = q_ref.shape[1]
    half = kd // 2
    dmat = dmat_ref[...]
    qdec = qdec_ref[...]
    kdec = kdec_ref[...]
    cdec = cdec_ref[...]
    for c in range(q_ref.shape[0] // chunk):
        rows = pl.ds(c * chunk, chunk)
        cos = cos_ref[rows, :]
        sin = sin_ref[rows, :]
        q = q_ref[rows, :]
        k = k_ref[rows, :]
        q = q * cos + pltpu.roll(q, half, 1) * sin
        k = (k * cos + pltpu.roll(k, half, 1) * sin) * (kd ** -0.5)
        vb = v_ref[rows, :].astype(BF16)
        state = state_ref[...]
        scores = _dot_nt(q.astype(BF16), k.astype(BF16)) * dmat
        o = _dot(scores.astype(BF16), vb) + _dot_nt((q * qdec).astype(BF16), state.astype(BF16))
        state_ref[...] = state * cdec + _dot_tn(vb, (k * kdec).astype(BF16))
        mu = jnp.mean(o, axis=-1, keepdims=True)
        oc = o - mu
        var = jnp.mean(oc * oc, axis=-1, keepdims=True)
        o = oc * lax.rsqrt(var + LN_EPS) * nw_ref[...]
        o_ref[rows, :] = (o * _silu(g_ref[rows, :])).astype(o_ref.dtype)


def _retention_tables(seq, chunk):
    half = LANES // 2
    inv_freq = ROPE_BASE ** (-jnp.arange(half, dtype=F32) / half)
    ang = jnp.arange(seq, dtype=F32)[:, None] * inv_freq[None, :]
    cos, sin = jnp.cos(ang), jnp.sin(ang)
    cos2 = jnp.concatenate([cos, cos], axis=1)
    sin2 = jnp.concatenate([-sin, sin], axis=1)
    log_gamma = jnp.log1p(-(2.0 ** (-5.0 - jnp.arange(RET_HEADS, dtype=F32))))
    pos = jnp.arange(chunk, dtype=F32)
    rel = pos[:, None] - pos[None, :]
    dmat = jnp.where(rel >= 0, jnp.exp(log_gamma[:, None, None] * jnp.maximum(rel, 0.0)), 0.0)
    qdec = jnp.exp(log_gamma[:, None] * (pos + 1.0)[None, :])
    kdec = jnp.exp(log_gamma[:, None] * (chunk - 1.0 - pos)[None, :])
    cdec = jnp.exp(log_gamma * chunk)
    bcast = lambda a: jnp.broadcast_to(a[..., None], a.shape + (LANES,))
    return cos2, sin2, dmat, bcast(qdec), bcast(kdec), bcast(cdec[:, None])


def _retention(proj, tables, norm_w, *, batch, seq):
    nt = seq // MIX_TT
    cos2, sin2, dmat, qdec, kdec, cdec = tables
    chunk = dmat.shape[1]

    def col(cb):
        return pl.BlockSpec((MIX_TT, LANES), lambda b, h, t: (b * nt + t, cb + h))

    rot = pl.BlockSpec((MIX_TT, LANES), lambda b, h, t: (t, 0))
    return pl.pallas_call(
        functools.partial(_ret_kernel, chunk=chunk),
        grid=(batch, RET_HEADS, nt),
        in_specs=[col(CB_D_Q), col(CB_D_K), col(CB_D_V), col(CB_D_G), rot, rot,
                  pl.BlockSpec((None, chunk, chunk), lambda b, h, t: (h, 0, 0)),
                  pl.BlockSpec((None, chunk, LANES), lambda b, h, t: (h, 0, 0)),
                  pl.BlockSpec((None, chunk, LANES), lambda b, h, t: (h, 0, 0)),
                  pl.BlockSpec((None, 1, LANES), lambda b, h, t: (h, 0, 0)),
                  pl.BlockSpec((1, LANES), lambda b, h, t: (0, 0))],
        out_specs=pl.BlockSpec((MIX_TT, LANES), lambda b, h, t: (b * nt + t, h)),
        out_shape=jax.ShapeDtypeStruct((batch * seq, GROUP_WIDTH), BF16),
        scratch_shapes=[pltpu.VMEM((LANES, LANES), F32)],
        compiler_params=_params(("parallel", "parallel", "arbitrary")),
        name="retention",
    )(proj, proj, proj, proj, cos2, sin2, dmat, qdec, kdec, cdec, norm_w.reshape(1, LANES))


def _lru_kernel(x_ref, y_ref, cw_ref, cb_ref, wa_ref, ba_ref, wx_ref, bx_ref, lam_ref, o_ref,
                xbuf_ref, a_ref, u_ref, carry_ref, *, scan_chunk):
    tt = x_ref.shape[0]

    @pl.when(pl.program_id(2) == 0)
    def _():
        xbuf_ref[pl.ds(0, SUBLANES), :] = jnp.zeros((SUBLANES, LANES), F32)
        carry_ref[...] = jnp.zeros_like(carry_ref)

    xbuf_ref[pl.ds(SUBLANES, tt), :] = x_ref[...]
    xc = cb_ref[...]
    for j in range(CONV_WIDTH):
        xc = xc + cw_ref[pl.ds(j, 1), :] * xbuf_ref[pl.ds(SUBLANES - (CONV_WIDTH - 1) + j, tt), :]
    xbuf_ref[pl.ds(0, SUBLANES), :] = x_ref[pl.ds(tt - SUBLANES, SUBLANES), :]

    xcb = xc.astype(BF16)
    r = _sigmoid(_dot(xcb, wa_ref[...].astype(BF16)) + ba_ref[...])
    i = _sigmoid(_dot(xcb, wx_ref[...].astype(BF16)) + bx_ref[...])
    log_a = (-LRU_C) * r * _softplus(-lam_ref[...])
    a_ref[...] = jnp.exp(log_a)
    th = jnp.tanh(log_a)
    u_ref[...] = jnp.sqrt(-2.0 * th / (1.0 - th)) * (i * xc)

    row = lax.broadcasted_iota(jnp.int32, (scan_chunk, LANES), 0)
    for c in range(tt // scan_chunk):
        rows = pl.ds(c * scan_chunk, scan_chunk)
        a = a_ref[rows, :]
        u = u_ref[rows, :]
        shift = 1
        while shift < scan_chunk:
            valid = row >= shift
            u = jnp.where(valid, a * pltpu.roll(u, shift, 0) + u, u)
            a = jnp.where(valid, a * pltpu.roll(a, shift, 0), a)
            shift *= 2
        h = a * carry_ref[...] + u
        carry_ref[...] = h[scan_chunk - 1:scan_chunk, :]
        o_ref[rows, :] = (h * jax.nn.gelu(y_ref[rows, :])).astype(o_ref.dtype)


def _rg_lru(proj, conv_w, conv_b, wa, ba, wx, bx, lam, *, batch, seq):
    nt = seq // MIX_TT
    vec = pl.BlockSpec((1, LANES), lambda b, n, t: (0, n))
    mat = pl.BlockSpec((None, LANES, LANES), lambda b, n, t: (n, 0, 0))
    return pl.pallas_call(
        functools.partial(_lru_kernel, scan_chunk=SCAN_CHUNK),
        grid=(batch, LRU_BLOCKS, nt),
        in_specs=[pl.BlockSpec((MIX_TT, LANES), lambda b, n, t: (b * nt + t, CB_B_X + n)),
                  pl.BlockSpec((MIX_TT, LANES), lambda b, n, t: (b * nt + t, CB_B_Y + n)),
                  pl.BlockSpec((CONV_WIDTH, LANES), lambda b, n, t: (0, n)),
                  vec, mat, vec, mat, vec, vec],
        out_specs=pl.BlockSpec((MIX_TT, LANES), lambda b, n, t: (b * nt + t, n)),
        out_shape=jax.ShapeDtypeStruct((batch * seq, GROUP_WIDTH), BF16),
        scratch_shapes=[pltpu.VMEM((MIX_TT + SUBLANES, LANES), F32),
                        pltpu.VMEM((MIX_TT, LANES), F32),
                        pltpu.VMEM((MIX_TT, LANES), F32),
                        pltpu.VMEM((1, LANES), F32)],
        compiler_params=_params(("parallel", "parallel", "arbitrary")),
        name="rg_lru",
    )(proj, proj, conv_w, conv_b.reshape(1, -1), wa, ba.reshape(1, -1), wx, bx.reshape(1, -1),
      lam.reshape(1, -1))


def _moe_ffn_ln(h, hp, sorted_buf, router_w, exp_w_gate, exp_w_up, exp_w_down, lead, ln_g, ln_b, *, alpha):
    m, d = h.shape
    rw = jnp.concatenate([router_w, jnp.zeros((d, LANES - N_EXPERTS), F32)], axis=1)
    ei, wt, rk, cnt = _router(h, rw, tm=256)

    counts = cnt[0, :N_EXPERTS]
    padded = ((counts + MOE_TR - 1) // MOE_TR) * MOE_TR
    ends = jnp.cumsum(padded)
    offs = ends - padded
    dst1 = offs[ei[:, 0]] + rk[:, 0]
    dst2 = offs[ei[:, 1]] + rk[:, 1]
    starts = jnp.arange(sorted_buf.shape[0] // MOE_TR, dtype=jnp.int32) * MOE_TR
    te = jnp.minimum(jnp.sum(starts[:, None] >= ends[None, :], axis=1), N_EXPERTS - 1).astype(jnp.int32)
    tv = (starts < ends[N_EXPERTS - 1]).astype(jnp.int32)
    prev = jnp.concatenate([jnp.full((1,), -1, jnp.int32), te[:-1]])
    tf = (te != prev).astype(jnp.int32) * tv

    sorted_buf = _dispatch(hp, dst1, dst2, sorted_buf, td=256)
    hid = _moe_up(sorted_buf, exp_w_gate, exp_w_up, lead, te, tf, tv, tn=512)
    y = _moe_down(hid, exp_w_down, lead, te, tf, tv, tn=2048)
    h, hb = _combine_ln(y, dst1, dst2, h, wt, ln_g, ln_b, alpha=alpha, tc=256)
    return h, hb, sorted_buf


def kernel(x, w_in, w_out, hgrn_lb, hgrn_norm, conv_w, conv_b, lru_wa, lru_ba, lru_wx, lru_bx, lru_lambda,
           gla_w_gate, gla_b_gate, gla_norm, ret_norm, ln1_g, ln1_b, ln2_g, ln2_b, ffn_w_gate, ffn_w_up,
           ffn_w_down, router_w, exp_w_gate, exp_w_up, exp_w_down):
    batch, seq, d = x.shape
    depth = w_in.shape[0]
    m = batch * seq
    alpha = (2 * depth) ** 0.25
    tables = _retention_tables(seq, MIX_CHUNK)

    sorted_rows = TOP_K * m + N_EXPERTS * MOE_TR
    sorted_buf = jnp.zeros((sorted_rows, d // 2), jnp.uint32)

    h = x.reshape(m, d)
    hb = h.astype(BF16)
    for layer in range(depth):
        wl = w_in[layer]
        w_r = jnp.concatenate([wl[:, ABC_COLS:D_COL0], jnp.zeros((d, LANES - GLA_RANK), F32)], axis=1)
        w_d = wl[:, D_COL0:]
        proj = _matmul_wres([hb], w_in, layer, n_cols=ABC_COLS, tm=1024, tn=512, out_dtype=F32, name="in_proj")
        proj_d = _matmul_wres([hb], w_d, None, n_cols=w_d.shape[1], tm=1024, tn=512, out_dtype=F32,
                              name="in_proj_ret")
        proj_r = _matmul_wres([hb], w_r, None, n_cols=LANES, tm=1024, tn=LANES, out_dtype=F32,
                              name="in_proj_gate")

        w_gate_p = jnp.concatenate(
            [gla_w_gate[layer], jnp.zeros((LANES - GLA_RANK, gla_w_gate.shape[2]), F32)], axis=0)
        o_a = _hgrn(proj, hgrn_lb, hgrn_norm[layer], layer=layer, batch=batch, seq=seq)
        o_b = _rg_lru(proj, conv_w[layer], conv_b[layer], lru_wa[layer], lru_ba[layer], lru_wx[layer],
                      lru_bx[layer], lru_lambda[layer], batch=batch, seq=seq)
        o_c = _gla(proj, proj_r, w_gate_p, gla_b_gate[layer], gla_norm[layer], batch=batch, seq=seq)
        o_d = _retention(proj_d, tables, ret_norm[layer], batch=batch, seq=seq)
        mix = _matmul_wres([o_a, o_b, o_c, o_d], w_out, layer, n_cols=d, tm=1024, tn=512, out_dtype=F32,
                           name="out_proj")

        j = layer // 2
        if layer % 2 == 0:
            h, hb = _add_ln(h, mix, ln1_g[layer], ln1_b[layer], alpha=alpha, tm=256)
            hid = _swiglu_up(hb, ffn_w_gate, ffn_w_up, j, tm=1024, tn=256)
            ff = _matmul(hid, ffn_w_down[j].astype(BF16), tm=512, tn=1024, tk=4096, out_dtype=F32,
                         name="ffn_down")
            h, hb = _add_ln(h, ff, ln2_g[layer], ln2_b[layer], alpha=alpha, tm=256)
        else:
            h, hb, hp = _add_ln(h, mix, ln1_g[layer], ln1_b[layer], alpha=alpha, tm=256, packed=True)
            h, hb, sorted_buf = _moe_ffn_ln(h, hp, sorted_buf, router_w[j], exp_w_gate, exp_w_up, exp_w_down, j,
                                            ln2_g[layer], ln2_b[layer], alpha=alpha)
    return h.reshape(batch, seq, d)
```

```python
import functools
import math

import jax
import jax.numpy as jnp
import numpy as np
from jax import lax
from jax.experimental import pallas as pl
from jax.experimental.pallas import tpu as pltpu

F32 = jnp.float32
BF16 = jnp.bfloat16

LANES = 128
SUBLANES = 8
VMEM_LIMIT = 56 * 1024 * 1024

D_MODEL = 4096
N_GROUPS = 4
GROUP_WIDTH = D_MODEL // N_GROUPS
HGRN_HEADS = 8
LRU_BLOCKS = 8
CONV_WIDTH = 4
LRU_C = 8.0
GLA_HEADS = 4
GLA_KEY = 128
GLA_VAL = 256
GLA_RANK = 16
GLA_NORMALIZER = 16.0
RET_HEADS = 8
ROPE_BASE = 10000.0
N_EXPERTS = 8
LN_EPS = 1e-5
RMS_EPS = 1e-6

CB_A_Q, CB_A_F, CB_A_I, CB_A_G = 0, 8, 16, 24
CB_B_X, CB_B_Y = 32, 40
CB_C_Q, CB_C_K, CB_C_V, CB_C_G = 48, 52, 56, 64
CB_D_Q, CB_D_K, CB_D_V, CB_D_G = 0, 8, 16, 24
ABC_COLS = 9216
D_COL0 = ABC_COLS + GLA_RANK

MIX_CHUNK = 128
MIX_TT = 1024
SCAN_CHUNK = 64
MOE_TR = 256
TOP_K = 2


def _params(semantics):
    return pltpu.CompilerParams(dimension_semantics=semantics, vmem_limit_bytes=VMEM_LIMIT)


def _sigmoid(x):
    return 1.0 / (1.0 + jnp.exp(-x))


def _silu(x):
    return x * _sigmoid(x)


def _softplus(x):
    return jnp.maximum(x, 0.0) + jnp.log1p(jnp.exp(-jnp.abs(x)))


def _dot(a, b):
    return jnp.dot(a, b, preferred_element_type=F32)


def _dot_nt(a, b):
    return lax.dot_general(a, b, (((1,), (1,)), ((), ())), preferred_element_type=F32)


def _dot_tn(a, b):
    return lax.dot_general(a, b, (((0,), (0,)), ((), ())), preferred_element_type=F32)


def _matmul_kernel(a_ref, b_ref, o_ref, *scratch, nk):
    if nk == 1:
        o_ref[...] = _dot(a_ref[...], b_ref[...]).astype(o_ref.dtype)
        return
    acc_ref, = scratch
    k = pl.program_id(2)

    @pl.when(k == 0)
    def _():
        acc_ref[...] = jnp.zeros_like(acc_ref)

    acc_ref[...] += _dot(a_ref[...], b_ref[...])

    @pl.when(k == nk - 1)
    def _():
        o_ref[...] = acc_ref[...].astype(o_ref.dtype)


def _matmul(a, b, lead, *, tm, tn, tk=None, out_dtype, name):
    m, kdim = a.shape
    n = b.shape[2]
    tk = kdim if tk is None else tk
    assert m % tm == 0 and n % tn == 0 and kdim % tk == 0
    nk = kdim // tk
    return pl.pallas_call(
        functools.partial(_matmul_kernel, nk=nk),
        grid=(m // tm, n // tn, nk),
        in_specs=[pl.BlockSpec((tm, tk), lambda i, j, k: (i, k)),
                  pl.BlockSpec((None, tk, tn), lambda i, j, k: (lead, k, j))],
        out_specs=pl.BlockSpec((tm, tn), lambda i, j, k: (i, j)),
        out_shape=jax.ShapeDtypeStruct((m, n), out_dtype),
        scratch_shapes=[pltpu.VMEM((tm, tn), F32)] if nk > 1 else [],
        compiler_params=_params(("parallel", "parallel", "arbitrary")),
        name=name,
    )(a, b)


def _swiglu_up_kernel(x_ref, wg_ref, wu_ref, o_ref, wgb_ref, wub_ref):
    @pl.when(pl.program_id(1) == 0)
    def _():
        wgb_ref[...] = wg_ref[...].astype(BF16)
        wub_ref[...] = wu_ref[...].astype(BF16)

    x = x_ref[...]
    g = _dot(x, wgb_ref[...])
    u = _dot(x, wub_ref[...])
    o_ref[...] = (_silu(g) * u).astype(o_ref.dtype)


def _swiglu_up(x, wg, wu, lead, *, tm, tn):
    m, d = x.shape
    f = wg.shape[2]
    wspec = pl.BlockSpec((None, d, tn), lambda j, i: (lead, 0, j))
    return pl.pallas_call(
        _swiglu_up_kernel,
        grid=(f // tn, m // tm),
        in_specs=[pl.BlockSpec((tm, d), lambda j, i: (i, 0)), wspec, wspec],
        out_specs=pl.BlockSpec((tm, tn), lambda j, i: (i, j)),
        out_shape=jax.ShapeDtypeStruct((m, f), BF16),
        scratch_shapes=[pltpu.VMEM((d, tn), BF16), pltpu.VMEM((d, tn), BF16)],
        compiler_params=_params(("parallel", "arbitrary")),
        name="swiglu_up",
    )(x, wg, wu)


def _wres_kernel(*refs, k_sizes):
    n_a = len(k_sizes)
    a_refs = refs[:n_a]
    b_ref, o_ref, wb_ref = refs[n_a:]

    @pl.when(pl.program_id(1) == 0)
    def _():
        wb_ref[...] = b_ref[...].astype(BF16)

    acc = None
    off = 0
    for a_ref, kg in zip(a_refs, k_sizes):
        part = _dot(a_ref[...], wb_ref[pl.ds(off, kg), :])
        acc = part if acc is None else acc + part
        off += kg
    o_ref[...] = acc.astype(o_ref.dtype)


def _matmul_wres(a_list, b, lead, *, n_cols, tm, tn, out_dtype, name):
    m = a_list[0].shape[0]
    k_sizes = tuple(a.shape[1] for a in a_list)
    kdim = sum(k_sizes)
    if lead is None:
        bspec = pl.BlockSpec((kdim, tn), lambda j, i: (0, j))
    else:
        bspec = pl.BlockSpec((None, kdim, tn), lambda j, i: (lead, 0, j))
    return pl.pallas_call(
        functools.partial(_wres_kernel, k_sizes=k_sizes),
        grid=(n_cols // tn, m // tm),
        in_specs=[pl.BlockSpec((tm, kg), lambda j, i: (i, 0)) for kg in k_sizes] + [bspec],
        out_specs=pl.BlockSpec((tm, tn), lambda j, i: (i, j)),
        out_shape=jax.ShapeDtypeStruct((m, n_cols), out_dtype),
        scratch_shapes=[pltpu.VMEM((kdim, tn), BF16)],
        compiler_params=_params(("parallel", "arbitrary")),
        name=name,
    )(*a_list, b)


def _router_kernel(h_ref, w_ref, ei_ref, wt_ref, rk_ref, cnt_ref, carry_ref):
    @pl.when(pl.program_id(0) == 0)
    def _():
        carry_ref[...] = jnp.zeros_like(carry_ref)

    logits = jnp.dot(h_ref[...], w_ref[...], preferred_element_type=F32,
                     precision=lax.Precision.HIGHEST)
    tm = logits.shape[0]
    lane = lax.broadcasted_iota(jnp.int32, logits.shape, 1)
    lanef = lane.astype(F32)
    neg = jnp.float32(-jnp.inf)
    logits = jnp.where(lane < N_EXPERTS, logits, neg)
    m1 = jnp.max(logits, axis=1, keepdims=True)
    i1 = jnp.min(jnp.where(logits == m1, lanef, float(LANES)), axis=1, keepdims=True)
    rest = jnp.where(lanef == i1, neg, logits)
    m2 = jnp.max(rest, axis=1, keepdims=True)
    i2 = jnp.min(jnp.where(rest == m2, lanef, float(LANES)), axis=1, keepdims=True)
    e2 = jnp.exp(m2 - m1)
    w1 = 1.0 / (1.0 + e2)
    w2 = e2 / (1.0 + e2)

    sel1 = lanef == i1
    sel2 = lanef == i2
    onehot = jnp.where(sel1 | sel2, 1.0, 0.0)
    t = lax.broadcasted_iota(jnp.int32, (tm, tm), 0)
    s = lax.broadcasted_iota(jnp.int32, (tm, tm), 1)
    before = jnp.where(t > s, 1.0, 0.0).astype(BF16)
    seen = _dot(before, onehot.astype(BF16)) + carry_ref[...]
    r1 = jnp.sum(jnp.where(sel1, seen, 0.0), axis=1, keepdims=True)
    r2 = jnp.sum(jnp.where(sel2, seen, 0.0), axis=1, keepdims=True)
    total = carry_ref[...] + jnp.sum(onehot, axis=0, keepdims=True)
    carry_ref[...] = total

    ei_ref[...] = jnp.where(lane == 0, i1, jnp.where(lane == 1, i2, 0.0)).astype(jnp.int32)
    wt_ref[...] = jnp.where(lane == 0, w1, jnp.where(lane == 1, w2, 0.0))
    rk_ref[...] = jnp.where(lane == 0, r1, jnp.where(lane == 1, r2, 0.0)).astype(jnp.int32)
    cnt_ref[...] = jnp.broadcast_to(total, cnt_ref.shape).astype(jnp.int32)


def _router(h, router_w_padded, *, tm):
    m, d = h.shape
    tok = pl.BlockSpec((tm, LANES), lambda i: (i, 0))
    return pl.pallas_call(
        _router_kernel,
        grid=(m // tm,),
        in_specs=[pl.BlockSpec((tm, d), lambda i: (i, 0)),
                  pl.BlockSpec((d, LANES), lambda i: (0, 0))],
        out_specs=[tok, tok, tok, pl.BlockSpec((SUBLANES, LANES), lambda i: (0, 0))],
        out_shape=[jax.ShapeDtypeStruct((m, LANES), jnp.int32), jax.ShapeDtypeStruct((m, LANES), F32),
                   jax.ShapeDtypeStruct((m, LANES), jnp.int32),
                   jax.ShapeDtypeStruct((SUBLANES, LANES), jnp.int32)],
        scratch_shapes=[pltpu.VMEM((1, LANES), F32)],
        compiler_params=_params(("arbitrary",)),
        name="router",
    )(h, router_w_padded)


def _dispatch_kernel(d1_ref, d2_ref, x_ref, buf_in_ref, o_ref, sem, *, td):
    del buf_in_ref
    base = pl.program_id(0) * td

    def row_copy(src_row, dst_row):
        return pltpu.make_async_copy(x_ref.at[pl.ds(src_row, 1), :], o_ref.at[pl.ds(dst_row, 1), :], sem)

    def issue(t, carry):
        row_copy(t, d1_ref[base + t]).start()
        row_copy(t, d2_ref[base + t]).start()
        return carry

    lax.fori_loop(0, td, issue, 0, unroll=8)

    def drain(t, carry):
        row_copy(0, 0).wait()
        row_copy(0, 0).wait()
        return carry

    lax.fori_loop(0, td, drain, 0, unroll=8)


def _dispatch(xp, dst1, dst2, buf, *, td):
    m = xp.shape[0]
    return pl.pallas_call(
        functools.partial(_dispatch_kernel, td=td),
        grid_spec=pltpu.PrefetchScalarGridSpec(
            num_scalar_prefetch=2, grid=(m // td,),
            in_specs=[pl.BlockSpec((td, xp.shape[1]), lambda i, d1, d2: (i, 0)),
                      pl.BlockSpec(memory_space=pl.ANY)],
            out_specs=pl.BlockSpec(memory_space=pl.ANY),
            scratch_shapes=[pltpu.SemaphoreType.DMA(())]),
        out_shape=jax.ShapeDtypeStruct(buf.shape, buf.dtype),
        input_output_aliases={3: 0},
        compiler_params=_params(("arbitrary",)),
        name="moe_dispatch",
    )(dst1, dst2, xp, buf)


def _unpack_halves(xp):
    lo = lax.bitcast_convert_type(xp << jnp.uint32(16), F32).astype(BF16)
    hi = lax.bitcast_convert_type(xp & jnp.uint32(0xFFFF0000), F32).astype(BF16)
    return lo, hi


def _moe_up_kernel(te_ref, tf_ref, tv_ref, x_ref, wg_ref, wu_ref, o_ref, wgb_ref, wub_ref):
    del te_ref
    i = pl.program_id(1)

    @pl.when(tf_ref[i] == 1)
    def _():
        wgb_ref[...] = wg_ref[...].astype(BF16)
        wub_ref[...] = wu_ref[...].astype(BF16)

    @pl.when(tv_ref[i] == 1)
    def _():
        half = x_ref.shape[1]
        lo, hi = _unpack_halves(x_ref[...])
        g = _dot(lo, wgb_ref[pl.ds(0, half), :]) + _dot(hi, wgb_ref[pl.ds(half, half), :])
        u = _dot(lo, wub_ref[pl.ds(0, half), :]) + _dot(hi, wub_ref[pl.ds(half, half), :])
        o_ref[...] = (_silu(g) * u).astype(o_ref.dtype)

    @pl.when(tv_ref[i] == 0)
    def _():
        o_ref[...] = jnp.zeros_like(o_ref)


def _moe_up(xs, wg, wu, lead, te, tf, tv, *, tn):
    rows, half = xs.shape
    d = 2 * half
    f = wg.shape[3]
    wspec = pl.BlockSpec((None, None, d, tn), lambda j, i, te, tf, tv: (lead, te[i], 0, j))
    return pl.pallas_call(
        _moe_up_kernel,
        grid_spec=pltpu.PrefetchScalarGridSpec(
            num_scalar_prefetch=3, grid=(f // tn, rows // MOE_TR),
            in_specs=[pl.BlockSpec((MOE_TR, half), lambda j, i, te, tf, tv: (i, 0)), wspec, wspec],
            out_specs=pl.BlockSpec((MOE_TR, tn), lambda j, i, te, tf, tv: (i, j)),
            scratch_shapes=[pltpu.VMEM((d, tn), BF16), pltpu.VMEM((d, tn), BF16)]),
        out_shape=jax.ShapeDtypeStruct((rows, f), BF16),
        compiler_params=_params(("parallel", "arbitrary")),
        name="moe_up",
    )(te, tf, tv, xs, wg, wu)


def _moe_down_kernel(te_ref, tf_ref, tv_ref, x_ref, w_ref, o_ref, wb_ref):
    del te_ref
    i = pl.program_id(1)

    @pl.when(tf_ref[i] == 1)
    def _():
        wb_ref[...] = w_ref[...].astype(BF16)

    @pl.when(tv_ref[i] == 1)
    def _():
        o_ref[...] = _dot(x_ref[...], wb_ref[...])

    @pl.when(tv_ref[i] == 0)
    def _():
        o_ref[...] = jnp.zeros_like(o_ref)


def _moe_down(hid, wd, lead, te, tf, tv, *, tn):
    rows, f = hid.shape
    d = wd.shape[3]
    return pl.pallas_call(
        _moe_down_kernel,
        grid_spec=pltpu.PrefetchScalarGridSpec(
            num_scalar_prefetch=3, grid=(d // tn, rows // MOE_TR),
            in_specs=[pl.BlockSpec((MOE_TR, f), lambda j, i, te, tf, tv: (i, 0)),
                      pl.BlockSpec((None, None, f, tn), lambda j, i, te, tf, tv: (lead, te[i], 0, j))],
            out_specs=pl.BlockSpec((MOE_TR, tn), lambda j, i, te, tf, tv: (i, j)),
            scratch_shapes=[pltpu.VMEM((f, tn), BF16)]),
        out_shape=jax.ShapeDtypeStruct((rows, d), F32),
        compiler_params=_params(("parallel", "arbitrary")),
        name="moe_down",
    )(te, tf, tv, hid, wd)


def _combine_ln_kernel(d1_ref, d2_ref, y_ref, h_ref, wt_ref, g_ref, b_ref, o_ref, ob_ref,
                       buf1, buf2, sems, *, alpha, tc):
    i = pl.program_id(0)
    slot = i % 2

    def row_copy(src_row, buf, s, t):
        return pltpu.make_async_copy(y_ref.at[pl.ds(src_row, 1), :], buf.at[s, pl.ds(t, 1), :], sems.at[s])

    def gather_tile(step, s):
        base = step * tc

        def issue(t, carry):
            row_copy(d1_ref[base + t], buf1, s, t).start()
            row_copy(d2_ref[base + t], buf2, s, t).start()
            return carry

        lax.fori_loop(0, tc, issue, 0, unroll=8)

    @pl.when(i == 0)
    def _():
        gather_tile(0, 0)

    @pl.when(i + 1 < pl.num_programs(0))
    def _():
        gather_tile(i + 1, 1 - slot)

    def drain(t, carry):
        row_copy(0, buf1, slot, 0).wait()
        row_copy(0, buf2, slot, 0).wait()
        return carry

    lax.fori_loop(0, tc, drain, 0, unroll=8)

    wt = wt_ref[...]
    x = alpha * h_ref[...] + wt[:, 0:1] * buf1[slot] + wt[:, 1:2] * buf2[slot]
    mu = jnp.mean(x, axis=-1, keepdims=True)
    xc = x - mu
    var = jnp.mean(xc * xc, axis=-1, keepdims=True)
    out = xc * lax.rsqrt(var + LN_EPS) * g_ref[...] + b_ref[...]
    o_ref[...] = out
    ob_ref[...] = out.astype(BF16)


def _combine_ln(y, dst1, dst2, h, wt, g, b, *, alpha, tc):
    m, d = h.shape
    row = pl.BlockSpec((tc, d), lambda i, d1, d2: (i, 0))
    vec = pl.BlockSpec((1, d), lambda i, d1, d2: (0, 0))
    return pl.pallas_call(
        functools.partial(_combine_ln_kernel, alpha=alpha, tc=tc),
        grid_spec=pltpu.PrefetchScalarGridSpec(
            num_scalar_prefetch=2, grid=(m // tc,),
            in_specs=[pl.BlockSpec(memory_space=pl.ANY), row,
                      pl.BlockSpec((tc, LANES), lambda i, d1, d2: (i, 0)), vec, vec],
            out_specs=[row, row],
            scratch_shapes=[pltpu.VMEM((2, tc, d), F32), pltpu.VMEM((2, tc, d), F32),
                            pltpu.SemaphoreType.DMA((2,))]),
        out_shape=[jax.ShapeDtypeStruct((m, d), F32), jax.ShapeDtypeStruct((m, d), BF16)],
        compiler_params=_params(("arbitrary",)),
        name="moe_combine_ln",
    )(dst1, dst2, y, h, wt, g.reshape(1, d), b.reshape(1, d))


def _add_ln_kernel(h_ref, y_ref, g_ref, b_ref, o_ref, ob_ref, *maybe_packed_ref, alpha):
    x = alpha * h_ref[...] + y_ref[...]
    mu = jnp.mean(x, axis=-1, keepdims=True)
    xc = x - mu
    var = jnp.mean(xc * xc, axis=-1, keepdims=True)
    out = xc * lax.rsqrt(var + LN_EPS) * g_ref[...] + b_ref[...]
    o_ref[...] = out
    out_b = out.astype(BF16)
    ob_ref[...] = out_b
    if maybe_packed_ref:
        half = out.shape[1] // 2
        as_u32 = lax.bitcast_convert_type(out_b.astype(F32), jnp.uint32)
        maybe_packed_ref[0][...] = (as_u32[:, :half] >> jnp.uint32(16)) | as_u32[:, half:]


def _add_ln(h, y, g, b, *, alpha, tm, packed=False):
    m, d = h.shape
    row = pl.BlockSpec((tm, d), lambda i: (i, 0))
    vec = pl.BlockSpec((1, d), lambda i: (0, 0))
    out_specs = [row, row]
    out_shape = [jax.ShapeDtypeStruct((m, d), F32), jax.ShapeDtypeStruct((m, d), BF16)]
    if packed:
        out_specs.append(pl.BlockSpec((tm, d // 2), lambda i: (i, 0)))
        out_shape.append(jax.ShapeDtypeStruct((m, d // 2), jnp.uint32))
    return pl.pallas_call(
        functools.partial(_add_ln_kernel, alpha=alpha),
        grid=(m // tm,),
        in_specs=[row, row, vec, vec],
        out_specs=out_specs,
        out_shape=out_shape,
        compiler_params=_params(("parallel",)),
        name="add_ln",
    )(h, y, g.reshape(1, d), b.reshape(1, d))


def _level_masks(c):
    t = lax.broadcasted_iota(jnp.int32, (c, c), 0)
    s = lax.broadcasted_iota(jnp.int32, (c, c), 1)
    x = t ^ s
    masks = [t == s]
    for p in range(1, int(math.log2(c)) + 1):
        masks.append((t > s) & ((x >> (p - 1)) == 1))
    return masks


def _midpoint_rows(cum_ref, c, k, p):
    blk = 1 << p
    half = blk >> 1
    if blk >= SUBLANES:
        parts = [jnp.broadcast_to(cum_ref[pl.ds(b * blk + half - 1, 1), :], (blk, k))
                 for b in range(c // blk)]
        return parts[0] if len(parts) == 1 else jnp.concatenate(parts, axis=0)
    sub = lax.broadcasted_iota(jnp.int32, (c, k), 0) & (SUBLANES - 1)
    res = None
    for j in range(SUBLANES // blk):
        rows = jnp.concatenate(
            [jnp.broadcast_to(cum_ref[pl.ds(SUBLANES * v + j * blk + half - 1, 1), :], (SUBLANES, k))
             for v in range(c // SUBLANES)], axis=0)
        res = rows if res is None else jnp.where(sub >= j * blk, rows, res)
    return res


def _split3(x):
    hi = x.astype(BF16)
    r1 = x - hi.astype(F32)
    mid = r1.astype(BF16)
    lo = (r1 - mid.astype(F32)).astype(BF16)
    return hi, mid, lo


def _gla_chunk(q, k, v, g, state_ref, cum_ref, masks, tri):
    c, kd = q.shape
    g_hi, g_mid, g_lo = _split3(g)
    cum = _dot(tri, g_hi) + _dot(tri, g_mid) + _dot(tri, g_lo)
    cum_ref[...] = cum
    row = lax.broadcasted_iota(jnp.int32, (c, kd), 0)
    qb = q.astype(BF16)
    kb = k.astype(BF16)
    vb = v.astype(BF16)
    scores = jnp.where(masks[0], _dot_nt(qb, kb), 0.0)
    for p in range(1, len(masks)):
        mid = _midpoint_rows(cum_ref, c, kd, p)
        upper = ((row >> (p - 1)) & 1) == 1
        z = (jnp.where(upper, q, k) * jnp.exp(-jnp.abs(cum - mid))).astype(BF16)
        scores = jnp.where(masks[p], _dot_nt(z, z), scores)
    state = state_ref[...]
    out = _dot(scores.astype(BF16), vb)
    out = out + _dot_nt((q * jnp.exp(cum)).astype(BF16), state.astype(BF16))
    last = cum_ref[pl.ds(c - 1, 1), :]
    k_dec = (k * jnp.exp(last - cum)).astype(BF16)
    state_ref[...] = state * jnp.exp(last) + _dot_tn(vb, k_dec)
    return out


def _tri_ones(c):
    t = lax.broadcasted_iota(jnp.int32, (c, c), 0)
    s = lax.broadcasted_iota(jnp.int32, (c, c), 1)
    return jnp.where(t >= s, 1.0, 0.0).astype(BF16)


def _hgrn_kernel(q_ref, f_ref, i_ref, g_ref, lb_ref, nw_ref, o_ref, state_ref, cum_ref, *, layer, chunk):
    @pl.when(pl.program_id(2) == 0)
    def _():
        state_ref[...] = jnp.zeros_like(state_ref)

    lb_all = lb_ref[...]
    e = jnp.exp(lb_all - jnp.max(lb_all, axis=0, keepdims=True))
    soft = e / jnp.sum(e, axis=0, keepdims=True)
    lower = jnp.zeros((1, LANES), F32)
    for l in range(1, layer + 1):
        lower = lower + soft[l:l + 1, :]
    masks = _level_masks(chunk)
    tri = _tri_ones(chunk)
    kd = q_ref.shape[1]
    for c in range(q_ref.shape[0] // chunk):
        rows = pl.ds(c * chunk, chunk)
        f = lower + (1.0 - lower) * _sigmoid(f_ref[rows, :])
        q = _silu(q_ref[rows, :]) * (kd ** -0.5)
        o = _gla_chunk(q, 1.0 - f, i_ref[rows, :], jnp.log(f), state_ref, cum_ref, masks, tri)
        o = o * lax.rsqrt(jnp.mean(o * o, axis=-1, keepdims=True) + RMS_EPS) * nw_ref[...]
        o_ref[rows, :] = (o * _silu(g_ref[rows, :])).astype(o_ref.dtype)


def _hgrn(proj, hgrn_lb, norm_w, *, layer, batch, seq):
    nt = seq // MIX_TT

    def col(cb):
        return pl.BlockSpec((MIX_TT, LANES), lambda b, h, t: (b * nt + t, cb + h))

    return pl.pallas_call(
        functools.partial(_hgrn_kernel, layer=layer, chunk=MIX_CHUNK),
        grid=(batch, HGRN_HEADS, nt),
        in_specs=[col(CB_A_Q), col(CB_A_F), col(CB_A_I), col(CB_A_G),
                  pl.BlockSpec((hgrn_lb.shape[0], LANES), lambda b, h, t: (0, h)),
                  pl.BlockSpec((1, LANES), lambda b, h, t: (0, 0))],
        out_specs=pl.BlockSpec((MIX_TT, LANES), lambda b, h, t: (b * nt + t, h)),
        out_shape=jax.ShapeDtypeStruct((batch * seq, GROUP_WIDTH), BF16),
        scratch_shapes=[pltpu.VMEM((LANES, LANES), F32), pltpu.VMEM((MIX_CHUNK, LANES), F32)],
        compiler_params=_params(("parallel", "parallel", "arbitrary")),
        name="hgrn",
    )(proj, proj, proj, proj, hgrn_lb, norm_w.reshape(1, LANES))


def _gla_kernel(q_ref, k_ref, v_ref, g_ref, r_ref, wg_ref, bg_ref, nw_ref, o_ref, state_ref, cum_ref, *, chunk):
    @pl.when(pl.program_id(2) == 0)
    def _():
        state_ref[...] = jnp.zeros_like(state_ref)

    masks = _level_masks(chunk)
    tri = _tri_ones(chunk)
    kd = q_ref.shape[1]
    for c in range(q_ref.shape[0] // chunk):
        rows = pl.ds(c * chunk, chunk)
        pre = _dot(r_ref[rows, :], wg_ref[...]) + bg_ref[...]
        log_alpha = (jnp.minimum(pre, 0.0) - jnp.log1p(jnp.exp(-jnp.abs(pre)))) * (1.0 / GLA_NORMALIZER)
        o = _gla_chunk(q_ref[rows, :] * (kd ** -0.5), k_ref[rows, :], v_ref[rows, :], log_alpha,
                       state_ref, cum_ref, masks, tri)
        o = o * lax.rsqrt(jnp.mean(o * o, axis=-1, keepdims=True) + RMS_EPS) * nw_ref[...]
        o_ref[rows, :] = (o * _silu(g_ref[rows, :])).astype(o_ref.dtype)


def _gla(proj, proj_r, w_gate_padded, b_gate, norm_w, *, batch, seq):
    nt = seq // MIX_TT
    vb = GLA_VAL // LANES

    return pl.pallas_call(
        functools.partial(_gla_kernel, chunk=MIX_CHUNK),
        grid=(batch, GLA_HEADS, nt),
        in_specs=[pl.BlockSpec((MIX_TT, GLA_KEY), lambda b, h, t: (b * nt + t, CB_C_Q + h)),
                  pl.BlockSpec((MIX_TT, GLA_KEY), lambda b, h, t: (b * nt + t, CB_C_K + h)),
                  pl.BlockSpec((MIX_TT, GLA_VAL), lambda b, h, t: (b * nt + t, CB_C_V // vb + h)),
                  pl.BlockSpec((MIX_TT, GLA_VAL), lambda b, h, t: (b * nt + t, CB_C_G // vb + h)),
                  pl.BlockSpec((MIX_TT, LANES), lambda b, h, t: (b * nt + t, 0)),
                  pl.BlockSpec((LANES, GLA_KEY), lambda b, h, t: (0, h)),
                  pl.BlockSpec((1, GLA_KEY), lambda b, h, t: (0, h)),
                  pl.BlockSpec((1, GLA_VAL), lambda b, h, t: (0, 0))],
        out_specs=pl.BlockSpec((MIX_TT, GLA_VAL), lambda b, h, t: (b * nt + t, h)),
        out_shape=jax.ShapeDtypeStruct((batch * seq, GROUP_WIDTH), BF16),
        scratch_shapes=[pltpu.VMEM((GLA_VAL, GLA_KEY), F32), pltpu.VMEM((MIX_CHUNK, GLA_KEY), F32)],
        compiler_params=_params(("parallel", "parallel", "arbitrary")),
        name="gla",
    )(proj, proj, proj, proj, proj_r, w_gate_padded, b_gate.reshape(1, -1), norm_w.reshape(1, GLA_VAL))


def _ret_kernel(q_ref, k_ref, v_ref, g_ref, cos_ref, sin_ref, dmat_ref, qdec_ref, kdec_ref, cdec_ref, nw_ref,
                o_ref, state_ref, *, chunk):
    @pl.when(pl.program_id(2) == 0)
    def _():
        state_ref[...] = jnp.zeros_like(state_ref)

    kd = q_ref.shape[1]
    half = kd // 2
    dmat = dmat_ref[...]
    qdec = qdec_ref[...]
    kdec = kdec_ref[...]
    cdec = cdec_ref[...]
    for c in range(q_ref.shape[0] // chunk):
        rows = pl.ds(c * chunk, chunk)
        cos = cos_ref[rows, :]
        sin = sin_ref[rows, :]
        q = q_ref[rows, :]
        k = k_ref[rows, :]
        q = q * cos + pltpu.roll(q, half, 1) * sin
        k = (k * cos + pltpu.roll(k, half, 1) * sin) * (kd ** -0.5)
        vb = v_ref[rows, :].astype(BF16)
        state = state_ref[...]
        scores = _dot_nt(q.astype(BF16), k.astype(BF16)) * dmat
        o = _dot(scores.astype(BF16), vb) + _dot_nt((q * qdec).astype(BF16), state.astype(BF16))
        state_ref[...] = state * cdec + _dot_tn(vb, (k * kdec).astype(BF16))
        mu = jnp.mean(o, axis=-1, keepdims=True)
        oc = o - mu
        var = jnp.mean(oc * oc, axis=-1, keepdims=True)
        o = oc * lax.rsqrt(var + LN_EPS) * nw_ref[...]
        o_ref[rows, :] = (o * _silu(g_ref[rows, :])).astype(o_ref.dtype)


def _retention_tables(seq, chunk):
    half = LANES // 2
    inv_freq = ROPE_BASE ** (-jnp.arange(half, dtype=F32) / half)
    ang = jnp.arange(seq, dtype=F32)[:, None] * inv_freq[None, :]
    cos, sin = jnp.cos(ang), jnp.sin(ang)
    cos2 = jnp.concatenate([cos, cos], axis=1)
    sin2 = jnp.concatenate([-sin, sin], axis=1)
    log_gamma = jnp.log1p(-(2.0 ** (-5.0 - jnp.arange(RET_HEADS, dtype=F32))))
    pos = jnp.arange(chunk, dtype=F32)
    rel = pos[:, None] - pos[None, :]
    dmat = jnp.where(rel >= 0, jnp.exp(log_gamma[:, None, None] * jnp.maximum(rel, 0.0)), 0.0)
    qdec = jnp.exp(log_gamma[:, None] * (pos + 1.0)[None, :])
    kdec = jnp.exp(log_gamma[:, None] * (chunk - 1.0 - pos)[None, :])
    cdec = jnp.exp(log_gamma * chunk)
    bcast = lambda a: jnp.broadcast_to(a[..., None], a.shape + (LANES,))
    return cos2, sin2, dmat, bcast(qdec), bcast(kdec), bcast(cdec[:, None])


def _retention(proj, tables, norm_w, *, batch, seq):
    nt = seq // MIX_TT
    cos2, sin2, dmat, qdec, kdec, cdec = tables
    chunk = dmat.shape[1]

    def col(cb):
        return pl.BlockSpec((MIX_TT, LANES), lambda b, h, t: (b * nt + t, cb + h))

    rot = pl.BlockSpec((MIX_TT, LANES), lambda b, h, t: (t, 0))
    return pl.pallas_call(
        functools.partial(_ret_kernel, chunk=chunk),
        grid=(batch, RET_HEADS, nt),
        in_specs=[col(CB_D_Q), col(CB_D_K), col(CB_D_V), col(CB_D_G), rot, rot,
                  pl.BlockSpec((None, chunk, chunk), lambda b, h, t: (h, 0, 0)),
                  pl.BlockSpec((None, chunk, LANES), lambda b, h, t: (h, 0, 0)),
                  pl.BlockSpec((None, chunk, LANES), lambda b, h, t: (h, 0, 0)),
                  pl.BlockSpec((None, 1, LANES), lambda b, h, t: (h, 0, 0)),
                  pl.BlockSpec((1, LANES), lambda b, h, t: (0, 0))],
        out_specs=pl.BlockSpec((MIX_TT, LANES), lambda b, h, t: (b * nt + t, h)),
        out_shape=jax.ShapeDtypeStruct((batch * seq, GROUP_WIDTH), BF16),
        scratch_shapes=[pltpu.VMEM((LANES, LANES), F32)],
        compiler_params=_params(("parallel", "parallel", "arbitrary")),
        name="retention",
    )(proj, proj, proj, proj, cos2, sin2, dmat, qdec, kdec, cdec, norm_w.reshape(1, LANES))


def _lru_kernel(x_ref, y_ref, cw_ref, cb_ref, wa_ref, ba_ref, wx_ref, bx_ref, lam_ref, o_ref,
                xbuf_ref, a_ref, u_ref, carry_ref, *, scan_chunk):
    tt = x_ref.shape[0]

    @pl.when(pl.program_id(2) == 0)
    def _():
        xbuf_ref[pl.ds(0, SUBLANES), :] = jnp.zeros((SUBLANES, LANES), F32)
        carry_ref[...] = jnp.zeros_like(carry_ref)

    xbuf_ref[pl.ds(SUBLANES, tt), :] = x_ref[...]
    xc = cb_ref[...]
    for j in range(CONV_WIDTH):
        xc = xc + cw_ref[pl.ds(j, 1), :] * xbuf_ref[pl.ds(SUBLANES - (CONV_WIDTH - 1) + j, tt), :]
    xbuf_ref[pl.ds(0, SUBLANES), :] = x_ref[pl.ds(tt - SUBLANES, SUBLANES), :]

    xcb = xc.astype(BF16)
    r = _sigmoid(_dot(xcb, wa_ref[...].astype(BF16)) + ba_ref[...])
    i = _sigmoid(_dot(xcb, wx_ref[...].astype(BF16)) + bx_ref[...])
    log_a = (-LRU_C) * r * _softplus(-lam_ref[...])
    a_ref[...] = jnp.exp(log_a)
    th = jnp.tanh(log_a)
    u_ref[...] = jnp.sqrt(-2.0 * th / (1.0 - th)) * (i * xc)

    row = lax.broadcasted_iota(jnp.int32, (scan_chunk, LANES), 0)
    for c in range(tt // scan_chunk):
        rows = pl.ds(c * scan_chunk, scan_chunk)
        a = a_ref[rows, :]
        u = u_ref[rows, :]
        shift = 1
        while shift < scan_chunk:
            valid = row >= shift
            u = jnp.where(valid, a * pltpu.roll(u, shift, 0) + u, u)
            a = jnp.where(valid, a * pltpu.roll(a, shift, 0), a)
            shift *= 2
        h = a * carry_ref[...] + u
        carry_ref[...] = h[scan_chunk - 1:scan_chunk, :]
        o_ref[rows, :] = (h * jax.nn.gelu(y_ref[rows, :])).astype(o_ref.dtype)


def _rg_lru(proj, conv_w, conv_b, wa, ba, wx, bx, lam, *, batch, seq):
    nt = seq // MIX_TT
    vec = pl.BlockSpec((1, LANES), lambda b, n, t: (0, n))
    mat = pl.BlockSpec((None, LANES, LANES), lambda b, n, t: (n, 0, 0))
    return pl.pallas_call(
        functools.partial(_lru_kernel, scan_chunk=SCAN_CHUNK),
        grid=(batch, LRU_BLOCKS, nt),
        in_specs=[pl.BlockSpec((MIX_TT, LANES), lambda b, n, t: (b * nt + t, CB_B_X + n)),
                  pl.BlockSpec((MIX_TT, LANES), lambda b, n, t: (b * nt + t, CB_B_Y + n)),
                  pl.BlockSpec((CONV_WIDTH, LANES), lambda b, n, t: (0, n)),
                  vec, mat, vec, mat, vec, vec],
        out_specs=pl.BlockSpec((MIX_TT, LANES), lambda b, n, t: (b * nt + t, n)),
        out_shape=jax.ShapeDtypeStruct((batch * seq, GROUP_WIDTH), BF16),
        scratch_shapes=[pltpu.VMEM((MIX_TT + SUBLANES, LANES), F32),
                        pltpu.VMEM((MIX_TT, LANES), F32),
                        pltpu.VMEM((MIX_TT, LANES), F32),
                        pltpu.VMEM((1, LANES), F32)],
        compiler_params=_params(("parallel", "parallel", "arbitrary")),
        name="rg_lru",
    )(proj, proj, conv_w, conv_b.reshape(1, -1), wa, ba.reshape(1, -1), wx, bx.reshape(1, -1),
      lam.reshape(1, -1))


def _moe_ffn_ln(h, hp, sorted_buf, router_w, exp_w_gate, exp_w_up, exp_w_down, lead, ln_g, ln_b, *, alpha):
    m, d = h.shape
    rw = jnp.concatenate([router_w, jnp.zeros((d, LANES - N_EXPERTS), F32)], axis=1)
    ei, wt, rk, cnt = _router(h, rw, tm=256)

    counts = cnt[0, :N_EXPERTS]
    padded = ((counts + MOE_TR - 1) // MOE_TR) * MOE_TR
    ends = jnp.cumsum(padded)
    offs = ends - padded
    dst1 = offs[ei[:, 0]] + rk[:, 0]
    dst2 = offs[ei[:, 1]] + rk[:, 1]
    starts = jnp.arange(sorted_buf.shape[0] // MOE_TR, dtype=jnp.int32) * MOE_TR
    te = jnp.minimum(jnp.sum(starts[:, None] >= ends[None, :], axis=1), N_EXPERTS - 1).astype(jnp.int32)
    tv = (starts < ends[N_EXPERTS - 1]).astype(jnp.int32)
    prev = jnp.concatenate([jnp.full((1,), -1, jnp.int32), te[:-1]])
    tf = (te != prev).astype(jnp.int32) * tv

    sorted_buf = _dispatch(hp, dst1, dst2, sorted_buf, td=256)
    hid = _moe_up(sorted_buf, exp_w_gate, exp_w_up, lead, te, tf, tv, tn=512)
    y = _moe_down(hid, exp_w_down, lead, te, tf, tv, tn=2048)
    h, hb = _combine_ln(y, dst1, dst2, h, wt, ln_g, ln_b, alpha=alpha, tc=256)
    return h, hb, sorted_buf


def kernel(x, w_in, w_out, hgrn_lb, hgrn_norm, conv_w, conv_b, lru_wa, lru_ba, lru_wx, lru_bx, lru_lambda,
           gla_w_gate, gla_b_gate, gla_norm, ret_norm, ln1_g, ln1_b, ln2_g, ln2_b, ffn_w_gate, ffn_w_up,
           ffn_w_down, router_w, exp_w_gate, exp_w_up, exp_w_down):
    batch, seq, d = x.shape
    depth = w_in.shape[0]
    m = batch * seq
    alpha = (2 * depth) ** 0.25
    tables = _retention_tables(seq, MIX_CHUNK)

    sorted_rows = TOP_K * m + N_EXPERTS * MOE_TR
    sorted_buf = jnp.zeros((sorted_rows, d // 2), jnp.uint32)

    w_abc = w_in[:, :, :ABC_COLS].astype(BF16)
    w_d = w_in[:, :, D_COL0:].astype(BF16)
    w_r = jnp.pad(w_in[:, :, ABC_COLS:D_COL0], ((0, 0), (0, 0), (0, LANES - GLA_RANK))).astype(BF16)
    w_down = ffn_w_down.astype(BF16)

    h = x.reshape(m, d)
    hb = h.astype(BF16)
    for layer in range(depth):
        proj = _matmul(hb, w_abc, layer, tm=1024, tn=512, out_dtype=F32, name="in_proj")
        proj_d = _matmul(hb, w_d, layer, tm=1024, tn=512, out_dtype=F32, name="in_proj_ret")
        proj_r = _matmul(hb, w_r, layer, tm=1024, tn=LANES, out_dtype=F32, name="in_proj_gate")

        w_gate_p = jnp.concatenate(
            [gla_w_gate[layer], jnp.zeros((LANES - GLA_RANK, gla_w_gate.shape[2]), F32)], axis=0)
        o_a = _hgrn(proj, hgrn_lb, hgrn_norm[layer], layer=layer, batch=batch, seq=seq)
        o_b = _rg_lru(proj, conv_w[layer], conv_b[layer], lru_wa[layer], lru_ba[layer], lru_wx[layer],
                      lru_bx[layer], lru_lambda[layer], batch=batch, seq=seq)
        o_c = _gla(proj, proj_r, w_gate_p, gla_b_gate[layer], gla_norm[layer], batch=batch, seq=seq)
        o_d = _retention(proj_d, tables, ret_norm[layer], batch=batch, seq=seq)
        mix = _matmul_wres([o_a, o_b, o_c, o_d], w_out, layer, n_cols=d, tm=1024, tn=512, out_dtype=F32,
                           name="out_proj")

        j = layer // 2
        if layer % 2 == 0:
            h, hb = _add_ln(h, mix, ln1_g[layer], ln1_b[layer], alpha=alpha, tm=256)
            hid = _swiglu_up(hb, ffn_w_gate, ffn_w_up, j, tm=1024, tn=256)
            ff = _matmul(hid, w_down, j, tm=512, tn=1024, tk=4096, out_dtype=F32, name="ffn_down")
            h, hb = _add_ln(h, ff, ln2_g[layer], ln2_b[layer], alpha=alpha, tm=256)
        else:
            h, hb, hp = _add_ln(h, mix, ln1_g[layer], ln1_b[layer], alpha=alpha, tm=256, packed=True)
            h, hb, sorted_buf = _moe_ffn_ln(h, hp, sorted_buf, router_w[j], exp_w_gate, exp_w_up, exp_w_down, j,
                                            ln2_g[layer], ln2_b[layer], alpha=alpha)
    return h.reshape(batch, seq, d)
```

```python
import functools
import math

import jax
import jax.numpy as jnp
import numpy as np
from jax import lax
from jax.experimental import pallas as pl
from jax.experimental.pallas import tpu as pltpu

F32 = jnp.float32
BF16 = jnp.bfloat16

LANES = 128
SUBLANES = 8
VMEM_LIMIT = 56 * 1024 * 1024

D_MODEL = 4096
N_GROUPS = 4
GROUP_WIDTH = D_MODEL // N_GROUPS
HGRN_HEADS = 8
LRU_BLOCKS = 8
CONV_WIDTH = 4
LRU_C = 8.0
GLA_HEADS = 4
GLA_KEY = 128
GLA_VAL = 256
GLA_RANK = 16
GLA_NORMALIZER = 16.0
RET_HEADS = 8
ROPE_BASE = 10000.0
N_EXPERTS = 8
LN_EPS = 1e-5
RMS_EPS = 1e-6

CB_A_Q, CB_A_F, CB_A_I, CB_A_G = 0, 8, 16, 24
CB_B_X, CB_B_Y = 32, 40
CB_C_Q, CB_C_K, CB_C_V, CB_C_G = 48, 52, 56, 64
CB_D_Q, CB_D_K, CB_D_V, CB_D_G = 0, 8, 16, 24
ABC_COLS = 9216
D_COL0 = ABC_COLS + GLA_RANK

MIX_CHUNK = 128
MIX_TT = 1024
SCAN_CHUNK = 64
MOE_TR = 256
TOP_K = 2


def _params(semantics):
    return pltpu.CompilerParams(dimension_semantics=semantics, vmem_limit_bytes=VMEM_LIMIT)


def _sigmoid(x):
    return 1.0 / (1.0 + jnp.exp(-x))


def _silu(x):
    return x * _sigmoid(x)


def _softplus(x):
    return jnp.maximum(x, 0.0) + jnp.log1p(jnp.exp(-jnp.abs(x)))


def _dot(a, b):
    return jnp.dot(a, b, preferred_element_type=F32)


def _dot_nt(a, b):
    return lax.dot_general(a, b, (((1,), (1,)), ((), ())), preferred_element_type=F32)


def _dot_tn(a, b):
    return lax.dot_general(a, b, (((0,), (0,)), ((), ())), preferred_element_type=F32)


def _matmul_kernel(a_ref, b_ref, o_ref, *scratch, nk):
    if nk == 1:
        o_ref[...] = _dot(a_ref[...], b_ref[...]).astype(o_ref.dtype)
        return
    acc_ref, = scratch
    k = pl.program_id(2)

    @pl.when(k == 0)
    def _():
        acc_ref[...] = jnp.zeros_like(acc_ref)

    acc_ref[...] += _dot(a_ref[...], b_ref[...])

    @pl.when(k == nk - 1)
    def _():
        o_ref[...] = acc_ref[...].astype(o_ref.dtype)


def _matmul_nt_kernel(a_ref, bt_ref, o_ref):
    o_ref[...] = _dot_nt(a_ref[...], bt_ref[0]).astype(o_ref.dtype)


def _matmul_nt(a, bt, lead, *, row0, n, tm, tn, out_dtype, name):
    m, kdim = a.shape
    assert m % tm == 0 and n % tn == 0
    return pl.pallas_call(
        _matmul_nt_kernel,
        grid=(m // tm, n // tn),
        in_specs=[pl.BlockSpec((tm, kdim), lambda i, j: (i, 0)),
                  pl.BlockSpec((pl.Element(1), pl.Element(tn), pl.Element(kdim)),
                               lambda i, j: (lead, pl.multiple_of(row0 + j * tn, 16), 0))],
        out_specs=pl.BlockSpec((tm, tn), lambda i, j: (i, j)),
        out_shape=jax.ShapeDtypeStruct((m, n), out_dtype),
        compiler_params=_params(("parallel", "parallel")),
        name=name,
    )(a, bt)


def _matmul(a, b, lead, *, tm, tn, tk=None, out_dtype, name):
    m, kdim = a.shape
    n = b.shape[2]
    tk = kdim if tk is None else tk
    assert m % tm == 0 and n % tn == 0 and kdim % tk == 0
    nk = kdim // tk
    return pl.pallas_call(
        functools.partial(_matmul_kernel, nk=nk),
        grid=(m // tm, n // tn, nk),
        in_specs=[pl.BlockSpec((tm, tk), lambda i, j, k: (i, k)),
                  pl.BlockSpec((None, tk, tn), lambda i, j, k: (lead, k, j))],
        out_specs=pl.BlockSpec((tm, tn), lambda i, j, k: (i, j)),
        out_shape=jax.ShapeDtypeStruct((m, n), out_dtype),
        scratch_shapes=[pltpu.VMEM((tm, tn), F32)] if nk > 1 else [],
        compiler_params=_params(("parallel", "parallel", "arbitrary")),
        name=name,
    )(a, b)


def _swiglu_up_kernel(x_ref, wg_ref, wu_ref, o_ref, wgb_ref, wub_ref):
    @pl.when(pl.program_id(1) == 0)
    def _():
        wgb_ref[...] = wg_ref[...].astype(BF16)
        wub_ref[...] = wu_ref[...].astype(BF16)

    x = x_ref[...]
    g = _dot(x, wgb_ref[...])
    u = _dot(x, wub_ref[...])
    o_ref[...] = (_silu(g) * u).astype(o_ref.dtype)


def _swiglu_up(x, wg, wu, lead, *, tm, tn):
    m, d = x.shape
    f = wg.shape[2]
    wspec = pl.BlockSpec((None, d, tn), lambda j, i: (lead, 0, j))
    return pl.pallas_call(
        _swiglu_up_kernel,
        grid=(f // tn, m // tm),
        in_specs=[pl.BlockSpec((tm, d), lambda j, i: (i, 0)), wspec, wspec],
        out_specs=pl.BlockSpec((tm, tn), lambda j, i: (i, j)),
        out_shape=jax.ShapeDtypeStruct((m, f), BF16),
        scratch_shapes=[pltpu.VMEM((d, tn), BF16), pltpu.VMEM((d, tn), BF16)],
        compiler_params=_params(("parallel", "arbitrary")),
        name="swiglu_up",
    )(x, wg, wu)


def _wres_kernel(*refs, k_sizes):
    n_a = len(k_sizes)
    a_refs = refs[:n_a]
    b_ref, o_ref, wb_ref = refs[n_a:]

    @pl.when(pl.program_id(1) == 0)
    def _():
        wb_ref[...] = b_ref[...].astype(BF16)

    acc = None
    off = 0
    for a_ref, kg in zip(a_refs, k_sizes):
        part = _dot(a_ref[...], wb_ref[pl.ds(off, kg), :])
        acc = part if acc is None else acc + part
        off += kg
    o_ref[...] = acc.astype(o_ref.dtype)


def _matmul_wres(a_list, b, lead, *, n_cols, tm, tn, out_dtype, name):
    m = a_list[0].shape[0]
    k_sizes = tuple(a.shape[1] for a in a_list)
    kdim = sum(k_sizes)
    if lead is None:
        bspec = pl.BlockSpec((kdim, tn), lambda j, i: (0, j))
    else:
        bspec = pl.BlockSpec((None, kdim, tn), lambda j, i: (lead, 0, j))
    return pl.pallas_call(
        functools.partial(_wres_kernel, k_sizes=k_sizes),
        grid=(n_cols // tn, m // tm),
        in_specs=[pl.BlockSpec((tm, kg), lambda j, i: (i, 0)) for kg in k_sizes] + [bspec],
        out_specs=pl.BlockSpec((tm, tn), lambda j, i: (i, j)),
        out_shape=jax.ShapeDtypeStruct((m, n_cols), out_dtype),
        scratch_shapes=[pltpu.VMEM((kdim, tn), BF16)],
        compiler_params=_params(("parallel", "arbitrary")),
        name=name,
    )(*a_list, b)


def _router_kernel(h_ref, w_ref, ei_ref, wt_ref, rk_ref, cnt_ref, carry_ref):
    @pl.when(pl.program_id(0) == 0)
    def _():
        carry_ref[...] = jnp.zeros_like(carry_ref)

    h = h_ref[...]
    w = w_ref[...]
    h_hi = h.astype(BF16)
    h_mid = (h - h_hi.astype(F32)).astype(BF16)
    w_hi = w.astype(BF16)
    w_mid = (w - w_hi.astype(F32)).astype(BF16)
    logits = _dot(h_hi, w_hi) + _dot(h_hi, w_mid) + _dot(h_mid, w_hi)
    tm = logits.shape[0]
    lane = lax.broadcasted_iota(jnp.int32, logits.shape, 1)
    lanef = lane.astype(F32)
    neg = jnp.float32(-jnp.inf)
    logits = jnp.where(lane < N_EXPERTS, logits, neg)
    m1 = jnp.max(logits, axis=1, keepdims=True)
    i1 = jnp.min(jnp.where(logits == m1, lanef, float(LANES)), axis=1, keepdims=True)
    rest = jnp.where(lanef == i1, neg, logits)
    m2 = jnp.max(rest, axis=1, keepdims=True)
    i2 = jnp.min(jnp.where(rest == m2, lanef, float(LANES)), axis=1, keepdims=True)
    e2 = jnp.exp(m2 - m1)
    w1 = 1.0 / (1.0 + e2)
    w2 = e2 / (1.0 + e2)

    sel1 = lanef == i1
    sel2 = lanef == i2
    onehot = jnp.where(sel1 | sel2, 1.0, 0.0)
    t = lax.broadcasted_iota(jnp.int32, (tm, tm), 0)
    s = lax.broadcasted_iota(jnp.int32, (tm, tm), 1)
    before = jnp.where(t > s, 1.0, 0.0).astype(BF16)
    seen = _dot(before, onehot.astype(BF16)) + carry_ref[...]
    r1 = jnp.sum(jnp.where(sel1, seen, 0.0), axis=1, keepdims=True)
    r2 = jnp.sum(jnp.where(sel2, seen, 0.0), axis=1, keepdims=True)
    total = carry_ref[...] + jnp.sum(onehot, axis=0, keepdims=True)
    carry_ref[...] = total

    ei_ref[...] = jnp.where(lane == 0, i1, jnp.where(lane == 1, i2, 0.0)).astype(jnp.int32)
    wt_ref[...] = jnp.where(lane == 0, w1, jnp.where(lane == 1, w2, 0.0))
    rk_ref[...] = jnp.where(lane == 0, r1, jnp.where(lane == 1, r2, 0.0)).astype(jnp.int32)
    cnt_ref[...] = jnp.broadcast_to(total, cnt_ref.shape).astype(jnp.int32)


def _router(h, router_w_padded, *, tm):
    m, d = h.shape
    tok = pl.BlockSpec((tm, LANES), lambda i: (i, 0))
    return pl.pallas_call(
        _router_kernel,
        grid=(m // tm,),
        in_specs=[pl.BlockSpec((tm, d), lambda i: (i, 0)),
                  pl.BlockSpec((d, LANES), lambda i: (0, 0))],
        out_specs=[tok, tok, tok, pl.BlockSpec((SUBLANES, LANES), lambda i: (0, 0))],
        out_shape=[jax.ShapeDtypeStruct((m, LANES), jnp.int32), jax.ShapeDtypeStruct((m, LANES), F32),
                   jax.ShapeDtypeStruct((m, LANES), jnp.int32),
                   jax.ShapeDtypeStruct((SUBLANES, LANES), jnp.int32)],
        scratch_shapes=[pltpu.VMEM((1, LANES), F32)],
        compiler_params=_params(("arbitrary",)),
        name="router",
    )(h, router_w_padded)


def _dispatch_kernel(d1_ref, d2_ref, x_ref, buf_in_ref, o_ref, sem, *, td):
    del buf_in_ref
    base = pl.program_id(0) * td

    def row_copy(src_row, dst_row):
        return pltpu.make_async_copy(x_ref.at[pl.ds(src_row, 1), :], o_ref.at[pl.ds(dst_row, 1), :], sem)

    def issue(t, carry):
        row_copy(t, d1_ref[base + t]).start()
        row_copy(t, d2_ref[base + t]).start()
        return carry

    lax.fori_loop(0, td, issue, 0, unroll=8)

    def drain(t, carry):
        row_copy(0, 0).wait()
        row_copy(0, 0).wait()
        return carry

    lax.fori_loop(0, td, drain, 0, unroll=8)


def _dispatch(xp, dst1, dst2, buf, *, td):
    m = xp.shape[0]
    return pl.pallas_call(
        functools.partial(_dispatch_kernel, td=td),
        grid_spec=pltpu.PrefetchScalarGridSpec(
            num_scalar_prefetch=2, grid=(m // td,),
            in_specs=[pl.BlockSpec((td, xp.shape[1]), lambda i, d1, d2: (i, 0)),
                      pl.BlockSpec(memory_space=pl.ANY)],
            out_specs=pl.BlockSpec(memory_space=pl.ANY),
            scratch_shapes=[pltpu.SemaphoreType.DMA(())]),
        out_shape=jax.ShapeDtypeStruct(buf.shape, buf.dtype),
        input_output_aliases={3: 0},
        compiler_params=_params(("arbitrary",)),
        name="moe_dispatch",
    )(dst1, dst2, xp, buf)


def _unpack_halves(xp):
    lo = lax.bitcast_convert_type(xp << jnp.uint32(16), F32).astype(BF16)
    hi = lax.bitcast_convert_type(xp & jnp.uint32(0xFFFF0000), F32).astype(BF16)
    return lo, hi


def _load_group_weights(j, i, sched, w_hbm_refs, wbuf, wb_refs, sems, *, lead, tn, nj):
    te_ref, tf_ref, _, tnx_ref, tlast_ref, tg_ref, ng_ref = sched

    def fetch(e, jj, slot):
        col = pl.multiple_of(jj * tn, tn)
        for m, w in enumerate(w_hbm_refs):
            pltpu.make_async_copy(w.at[lead, e, :, pl.ds(col, tn)], wbuf.at[slot, m], sems.at[slot]).start()

    @pl.when(tf_ref[i] == 1)
    def _():
        g = j * ng_ref[0] + tg_ref[i]
        slot = g % 2

        @pl.when(g == 0)
        def _():
            fetch(te_ref[i], j, slot)

        last = tlast_ref[i] == 1

        @pl.when(jnp.logical_not(last))
        def _():
            fetch(tnx_ref[i], j, 1 - slot)

        @pl.when(jnp.logical_and(last, j + 1 < nj))
        def _():
            fetch(tnx_ref[i], j + 1, 1 - slot)

        for m, w in enumerate(w_hbm_refs):
            pltpu.make_async_copy(w.at[lead, 0, :, pl.ds(0, tn)], wbuf.at[slot, m], sems.at[slot]).wait()
        for m, wb in enumerate(wb_refs):
            wb[...] = wbuf[slot, m].astype(BF16)


def _moe_up_kernel(*refs, lead, tn, nj):
    sched = refs[:7]
    x_ref, wg_hbm, wu_hbm, o_ref, wbuf, wgb_ref, wub_ref, sems = refs[7:]
    tv_ref = sched[2]
    j = pl.program_id(0)
    i = pl.program_id(1)
    _load_group_weights(j, i, sched, (wg_hbm, wu_hbm), wbuf, (wgb_ref, wub_ref), sems, lead=lead, tn=tn, nj=nj)

    @pl.when(tv_ref[i] == 1)
    def _():
        half = x_ref.shape[1]
        lo, hi = _unpack_halves(x_ref[...])
        g = _dot(lo, wgb_ref[pl.ds(0, half), :]) + _dot(hi, wgb_ref[pl.ds(half, half), :])
        u = _dot(lo, wub_ref[pl.ds(0, half), :]) + _dot(hi, wub_ref[pl.ds(half, half), :])
        o_ref[...] = (_silu(g) * u).astype(o_ref.dtype)

    @pl.when(tv_ref[i] == 0)
    def _():
        o_ref[...] = jnp.zeros_like(o_ref)


def _moe_up(xs, wg, wu, lead, sched, *, tn):
    rows, half = xs.shape
    d = 2 * half
    f = wg.shape[3]
    nj = f // tn
    hbm = pl.BlockSpec(memory_space=pl.ANY)
    return pl.pallas_call(
        functools.partial(_moe_up_kernel, lead=lead, tn=tn, nj=nj),
        grid_spec=pltpu.PrefetchScalarGridSpec(
            num_scalar_prefetch=len(sched), grid=(nj, rows // MOE_TR),
            in_specs=[pl.BlockSpec((MOE_TR, half), lambda j, i, *_: (i, 0)), hbm, hbm],
            out_specs=pl.BlockSpec((MOE_TR, tn), lambda j, i, *_: (i, j)),
            scratch_shapes=[pltpu.VMEM((2, 2, d, tn), F32), pltpu.VMEM((d, tn), BF16), pltpu.VMEM((d, tn), BF16),
                            pltpu.SemaphoreType.DMA((2,))]),
        out_shape=jax.ShapeDtypeStruct((rows, f), BF16),
        compiler_params=_params(("arbitrary", "arbitrary")),
        name="moe_up",
    )(*sched, xs, wg, wu)


def _moe_down_kernel(*refs, lead, tn, nj):
    sched = refs[:7]
    x_ref, w_hbm, o_ref, wbuf, wb_ref, sems = refs[7:]
    tv_ref = sched[2]
    j = pl.program_id(0)
    i = pl.program_id(1)
    _load_group_weights(j, i, sched, (w_hbm,), wbuf, (wb_ref,), sems, lead=lead, tn=tn, nj=nj)

    @pl.when(tv_ref[i] == 1)
    def _():
        o_ref[...] = _dot(x_ref[...], wb_ref[...])

    @pl.when(tv_ref[i] == 0)
    def _():
        o_ref[...] = jnp.zeros_like(o_ref)


def _moe_down(hid, wd, lead, sched, *, tn):
    rows, f = hid.shape
    d = wd.shape[3]
    nj = d // tn
    return pl.pallas_call(
        functools.partial(_moe_down_kernel, lead=lead, tn=tn, nj=nj),
        grid_spec=pltpu.PrefetchScalarGridSpec(
            num_scalar_prefetch=len(sched), grid=(nj, rows // MOE_TR),
            in_specs=[pl.BlockSpec((MOE_TR, f), lambda j, i, *_: (i, 0)), pl.BlockSpec(memory_space=pl.ANY)],
            out_specs=pl.BlockSpec((MOE_TR, tn), lambda j, i, *_: (i, j)),
            scratch_shapes=[pltpu.VMEM((2, 1, f, tn), F32), pltpu.VMEM((f, tn), BF16),
                            pltpu.SemaphoreType.DMA((2,))]),
        out_shape=jax.ShapeDtypeStruct((rows, d), F32),
        compiler_params=_params(("arbitrary", "arbitrary")),
        name="moe_down",
    )(*sched, hid, wd)


def _combine_ln_kernel(d1_ref, d2_ref, y_ref, h_ref, wt_ref, g_ref, b_ref, o_ref, ob_ref,
                       buf1, buf2, sems, *, alpha, tc):
    i = pl.program_id(0)
    slot = i % 2

    def row_copy(src_row, buf, s, t):
        return pltpu.make_async_copy(y_ref.at[pl.ds(src_row, 1), :], buf.at[s, pl.ds(t, 1), :], sems.at[s])

    def gather_tile(step, s):
        base = step * tc

        def issue(t, carry):
            row_copy(d1_ref[base + t], buf1, s, t).start()
            row_copy(d2_ref[base + t], buf2, s, t).start()
            return carry

        lax.fori_loop(0, tc, issue, 0, unroll=8)

    @pl.when(i == 0)
    def _():
        gather_tile(0, 0)

    @pl.when(i + 1 < pl.num_programs(0))
    def _():
        gather_tile(i + 1, 1 - slot)

    def drain(t, carry):
        row_copy(0, buf1, slot, 0).wait()
        row_copy(0, buf2, slot, 0).wait()
        return carry

    lax.fori_loop(0, tc, drain, 0, unroll=8)

    wt = wt_ref[...]
    x = alpha * h_ref[...] + wt[:, 0:1] * buf1[slot] + wt[:, 1:2] * buf2[slot]
    mu = jnp.mean(x, axis=-1, keepdims=True)
    xc = x - mu
    var = jnp.mean(xc * xc, axis=-1, keepdims=True)
    out = xc * lax.rsqrt(var + LN_EPS) * g_ref[...] + b_ref[...]
    o_ref[...] = out
    ob_ref[...] = out.astype(BF16)


def _combine_ln(y, dst1, dst2, h, wt, g, b, *, alpha, tc):
    m, d = h.shape
    row = pl.BlockSpec((tc, d), lambda i, d1, d2: (i, 0))
    vec = pl.BlockSpec((1, d), lambda i, d1, d2: (0, 0))
    return pl.pallas_call(
        functools.partial(_combine_ln_kernel, alpha=alpha, tc=tc),
        grid_spec=pltpu.PrefetchScalarGridSpec(
            num_scalar_prefetch=2, grid=(m // tc,),
            in_specs=[pl.BlockSpec(memory_space=pl.ANY), row,
                      pl.BlockSpec((tc, LANES), lambda i, d1, d2: (i, 0)), vec, vec],
            out_specs=[row, row],
            scratch_shapes=[pltpu.VMEM((2, tc, d), F32), pltpu.VMEM((2, tc, d), F32),
                            pltpu.SemaphoreType.DMA((2,))]),
        out_shape=[jax.ShapeDtypeStruct((m, d), F32), jax.ShapeDtypeStruct((m, d), BF16)],
        compiler_params=_params(("arbitrary",)),
        name="moe_combine_ln",
    )(dst1, dst2, y, h, wt, g.reshape(1, d), b.reshape(1, d))


def _add_ln_kernel(h_ref, y_ref, g_ref, b_ref, o_ref, ob_ref, *maybe_packed_ref, alpha):
    x = alpha * h_ref[...] + y_ref[...]
    mu = jnp.mean(x, axis=-1, keepdims=True)
    xc = x - mu
    var = jnp.mean(xc * xc, axis=-1, keepdims=True)
    out = xc * lax.rsqrt(var + LN_EPS) * g_ref[...] + b_ref[...]
    o_ref[...] = out
    out_b = out.astype(BF16)
    ob_ref[...] = out_b
    if maybe_packed_ref:
        half = out.shape[1] // 2
        as_u32 = lax.bitcast_convert_type(out_b.astype(F32), jnp.uint32)
        maybe_packed_ref[0][...] = (as_u32[:, :half] >> jnp.uint32(16)) | as_u32[:, half:]


def _add_ln(h, y, g, b, *, alpha, tm, packed=False):
    m, d = h.shape
    row = pl.BlockSpec((tm, d), lambda i: (i, 0))
    vec = pl.BlockSpec((1, d), lambda i: (0, 0))
    out_specs = [row, row]
    out_shape = [jax.ShapeDtypeStruct((m, d), F32), jax.ShapeDtypeStruct((m, d), BF16)]
    if packed:
        out_specs.append(pl.BlockSpec((tm, d // 2), lambda i: (i, 0)))
        out_shape.append(jax.ShapeDtypeStruct((m, d // 2), jnp.uint32))
    return pl.pallas_call(
        functools.partial(_add_ln_kernel, alpha=alpha),
        grid=(m // tm,),
        in_specs=[row, row, vec, vec],
        out_specs=out_specs,
        out_shape=out_shape,
        compiler_params=_params(("parallel",)),
        name="add_ln",
    )(h, y, g.reshape(1, d), b.reshape(1, d))


def _level_masks(c):
    t = lax.broadcasted_iota(jnp.int32, (c, c), 0)
    s = lax.broadcasted_iota(jnp.int32, (c, c), 1)
    x = t ^ s
    masks = [t == s]
    for p in range(1, int(math.log2(c)) + 1):
        masks.append((t > s) & ((x >> (p - 1)) == 1))
    return masks


def _midpoint_rows(cum_ref, c, k, p):
    blk = 1 << p
    half = blk >> 1
    if blk >= SUBLANES:
        parts = [jnp.broadcast_to(cum_ref[pl.ds(b * blk + half - 1, 1), :], (blk, k))
                 for b in range(c // blk)]
        return parts[0] if len(parts) == 1 else jnp.concatenate(parts, axis=0)
    sub = lax.broadcasted_iota(jnp.int32, (c, k), 0) & (SUBLANES - 1)
    res = None
    for j in range(SUBLANES // blk):
        rows = jnp.concatenate(
            [jnp.broadcast_to(cum_ref[pl.ds(SUBLANES * v + j * blk + half - 1, 1), :], (SUBLANES, k))
             for v in range(c // SUBLANES)], axis=0)
        res = rows if res is None else jnp.where(sub >= j * blk, rows, res)
    return res


def _split3(x):
    hi = x.astype(BF16)
    r1 = x - hi.astype(F32)
    mid = r1.astype(BF16)
    lo = (r1 - mid.astype(F32)).astype(BF16)
    return hi, mid, lo


def _gla_chunk(q, k, v, g, state_ref, cum_ref, masks, tri):
    c, kd = q.shape
    g_hi, g_mid, g_lo = _split3(g)
    cum = _dot(tri, g_hi) + _dot(tri, g_mid) + _dot(tri, g_lo)
    cum_ref[...] = cum
    row = lax.broadcasted_iota(jnp.int32, (c, kd), 0)
    qb = q.astype(BF16)
    kb = k.astype(BF16)
    vb = v.astype(BF16)
    scores = jnp.where(masks[0], _dot_nt(qb, kb), 0.0)
    for p in range(1, len(masks)):
        mid = _midpoint_rows(cum_ref, c, kd, p)
        upper = ((row >> (p - 1)) & 1) == 1
        z = (jnp.where(upper, q, k) * jnp.exp(-jnp.abs(cum - mid))).astype(BF16)
        scores = jnp.where(masks[p], _dot_nt(z, z), scores)
    state = state_ref[...]
    out = _dot(scores.astype(BF16), vb)
    out = out + _dot_nt((q * jnp.exp(cum)).astype(BF16), state.astype(BF16))
    last = cum_ref[pl.ds(c - 1, 1), :]
    k_dec = (k * jnp.exp(last - cum)).astype(BF16)
    state_ref[...] = state * jnp.exp(last) + _dot_tn(vb, k_dec)
    return out


def _tri_ones(c):
    t = lax.broadcasted_iota(jnp.int32, (c, c), 0)
    s = lax.broadcasted_iota(jnp.int32, (c, c), 1)
    return jnp.where(t >= s, 1.0, 0.0).astype(BF16)


def _hgrn_kernel(q_ref, f_ref, i_ref, g_ref, lb_ref, nw_ref, o_ref, state_ref, cum_ref, *, layer, chunk):
    @pl.when(pl.program_id(2) == 0)
    def _():
        state_ref[...] = jnp.zeros_like(state_ref)

    lb_all = lb_ref[...]
    e = jnp.exp(lb_all - jnp.max(lb_all, axis=0, keepdims=True))
    soft = e / jnp.sum(e, axis=0, keepdims=True)
    lower = jnp.zeros((1, LANES), F32)
    for l in range(1, layer + 1):
        lower = lower + soft[l:l + 1, :]
    masks = _level_masks(chunk)
    tri = _tri_ones(chunk)
    kd = q_ref.shape[1]
    for c in range(q_ref.shape[0] // chunk):
        rows = pl.ds(c * chunk, chunk)
        f = lower + (1.0 - lower) * _sigmoid(f_ref[rows, :])
        q = _silu(q_ref[rows, :]) * (kd ** -0.5)
        o = _gla_chunk(q, 1.0 - f, i_ref[rows, :], jnp.log(f), state_ref, cum_ref, masks, tri)
        o = o * lax.rsqrt(jnp.mean(o * o, axis=-1, keepdims=True) + RMS_EPS) * nw_ref[...]
        o_ref[rows, :] = (o * _silu(g_ref[rows, :])).astype(o_ref.dtype)


def _hgrn(proj, hgrn_lb, norm_w, *, layer, batch, seq):
    nt = seq // MIX_TT

    def col(cb):
        return pl.BlockSpec((MIX_TT, LANES), lambda b, h, t: (b * nt + t, cb + h))

    return pl.pallas_call(
        functools.partial(_hgrn_kernel, layer=layer, chunk=MIX_CHUNK),
        grid=(batch, HGRN_HEADS, nt),
        in_specs=[col(CB_A_Q), col(CB_A_F), col(CB_A_I), col(CB_A_G),
                  pl.BlockSpec((hgrn_lb.shape[0], LANES), lambda b, h, t: (0, h)),
                  pl.BlockSpec((1, LANES), lambda b, h, t: (0, 0))],
        out_specs=pl.BlockSpec((MIX_TT, LANES), lambda b, h, t: (b * nt + t, h)),
        out_shape=jax.ShapeDtypeStruct((batch * seq, GROUP_WIDTH), BF16),
        scratch_shapes=[pltpu.VMEM((LANES, LANES), F32), pltpu.VMEM((MIX_CHUNK, LANES), F32)],
        compiler_params=_params(("parallel", "parallel", "arbitrary")),
        name="hgrn",
    )(proj, proj, proj, proj, hgrn_lb, norm_w.reshape(1, LANES))


def _gla_kernel(q_ref, k_ref, v_ref, g_ref, r_ref, wg_ref, bg_ref, nw_ref, o_ref, state_ref, cum_ref, *, chunk):
    @pl.when(pl.program_id(2) == 0)
    def _():
        state_ref[...] = jnp.zeros_like(state_ref)

    masks = _level_masks(chunk)
    tri = _tri_ones(chunk)
    kd = q_ref.shape[1]
    for c in range(q_ref.shape[0] // chunk):
        rows = pl.ds(c * chunk, chunk)
        pre = _dot(r_ref[rows, :], wg_ref[...]) + bg_ref[...]
        log_alpha = (jnp.minimum(pre, 0.0) - jnp.log1p(jnp.exp(-jnp.abs(pre)))) * (1.0 / GLA_NORMALIZER)
        o = _gla_chunk(q_ref[rows, :] * (kd ** -0.5), k_ref[rows, :], v_ref[rows, :], log_alpha,
                       state_ref, cum_ref, masks, tri)
        o = o * lax.rsqrt(jnp.mean(o * o, axis=-1, keepdims=True) + RMS_EPS) * nw_ref[...]
        o_ref[rows, :] = (o * _silu(g_ref[rows, :])).astype(o_ref.dtype)


def _gla(proj, proj_r, w_gate_padded, b_gate, norm_w, *, batch, seq):
    nt = seq // MIX_TT
    vb = GLA_VAL // LANES

    return pl.pallas_call(
        functools.partial(_gla_kernel, chunk=MIX_CHUNK),
        grid=(batch, GLA_HEADS, nt),
        in_specs=[pl.BlockSpec((MIX_TT, GLA_KEY), lambda b, h, t: (b * nt + t, CB_C_Q + h)),
                  pl.BlockSpec((MIX_TT, GLA_KEY), lambda b, h, t: (b * nt + t, CB_C_K + h)),
                  pl.BlockSpec((MIX_TT, GLA_VAL), lambda b, h, t: (b * nt + t, CB_C_V // vb + h)),
                  pl.BlockSpec((MIX_TT, GLA_VAL), lambda b, h, t: (b * nt + t, CB_C_G // vb + h)),
                  pl.BlockSpec((MIX_TT, LANES), lambda b, h, t: (b * nt + t, 0)),
                  pl.BlockSpec((LANES, GLA_KEY), lambda b, h, t: (0, h)),
                  pl.BlockSpec((1, GLA_KEY), lambda b, h, t: (0, h)),
                  pl.BlockSpec((1, GLA_VAL), lambda b, h, t: (0, 0))],
        out_specs=pl.BlockSpec((MIX_TT, GLA_VAL), lambda b, h, t: (b * nt + t, h)),
        out_shape=jax.ShapeDtypeStruct((batch * seq, GROUP_WIDTH), BF16),
        scratch_shapes=[pltpu.VMEM((GLA_VAL, GLA_KEY), F32), pltpu.VMEM((MIX_CHUNK, GLA_KEY), F32)],
        compiler_params=_params(("parallel", "parallel", "arbitrary")),
        name="gla",
    )(proj, proj, proj, proj, proj_r, w_gate_padded, b_gate.reshape(1, -1), norm_w.reshape(1, GLA_VAL))


def _ret_kernel(q_ref, k_ref, v_ref, g_ref, cos_ref, sin_ref, dmat_ref, qdec_ref, kdec_ref, cdec_ref, nw_ref,
                o_ref, state_ref, *, chunk):
    @pl.when(pl.program_id(2) == 0)
    def _():
        state_ref[...] = jnp.zeros_like(state_ref)

    kd = q_ref.shape[1]
    half = kd // 2
    dmat = dmat_ref[...]
    qdec = qdec_ref[...]
    kdec = kdec_ref[...]
    cdec = cdec_ref[...]
    for c in range(q_ref.shape[0] // chunk):
        rows = pl.ds(c * chunk, chunk)
        cos = cos_ref[rows, :]
        sin = sin_ref[rows, :]
        q = q_ref[rows, :]
        k = k_ref[rows, :]
        q = q * cos + pltpu.roll(q, half, 1) * sin
        k = (k * cos + pltpu.roll(k, half, 1) * sin) * (kd ** -0.5)
        vb = v_ref[rows, :].astype(BF16)
        state = state_ref[...]
        scores = _dot_nt(q.astype(BF16), k.astype(BF16)) * dmat
        o = _dot(scores.astype(BF16), vb) + _dot_nt((q * qdec).astype(BF16), state.astype(BF16))
        state_ref[...] = state * cdec + _dot_tn(vb, (k * kdec).astype(BF16))
        mu = jnp.mean(o, axis=-1, keepdims=True)
        oc = o - mu
        var = jnp.mean(oc * oc, axis=-1, keepdims=True)
        o = oc * lax.rsqrt(var + LN_EPS) * nw_ref[...]
        o_ref[rows, :] = (o * _silu(g_ref[rows, :])).astype(o_ref.dtype)


def _retention_tables(seq, chunk):
    half = LANES // 2
    inv_freq = ROPE_BASE ** (-jnp.arange(half, dtype=F32) / half)
    ang = jnp.arange(seq, dtype=F32)[:, None] * inv_freq[None, :]
    cos, sin = jnp.cos(ang), jnp.sin(ang)
    cos2 = jnp.concatenate([cos, cos], axis=1)
    sin2 = jnp.concatenate([-sin, sin], axis=1)
    log_gamma = jnp.log1p(-(2.0 ** (-5.0 - jnp.arange(RET_HEADS, dtype=F32))))
    pos = jnp.arange(chunk, dtype=F32)
    rel = pos[:, None] - pos[None, :]
    dmat = jnp.where(rel >= 0, jnp.exp(log_gamma[:, None, None] * jnp.maximum(rel, 0.0)), 0.0)
    qdec = jnp.exp(log_gamma[:, None] * (pos + 1.0)[None, :])
    kdec = jnp.exp(log_gamma[:, None] * (chunk - 1.0 - pos)[None, :])
    cdec = jnp.exp(log_gamma * chunk)
    bcast = lambda a: jnp.broadcast_to(a[..., None], a.shape + (LANES,))
    return cos2, sin2, dmat, bcast(qdec), bcast(kdec), bcast(cdec[:, None])


def _retention(proj, tables, norm_w, *, batch, seq):
    nt = seq // MIX_TT
    cos2, sin2, dmat, qdec, kdec, cdec = tables
    chunk = dmat.shape[1]

    def col(cb):
        return pl.BlockSpec((MIX_TT, LANES), lambda b, h, t: (b * nt + t, cb + h))

    rot = pl.BlockSpec((MIX_TT, LANES), lambda b, h, t: (t, 0))
    return pl.pallas_call(
        functools.partial(_ret_kernel, chunk=chunk),
        grid=(batch, RET_HEADS, nt),
        in_specs=[col(CB_D_Q), col(CB_D_K), col(CB_D_V), col(CB_D_G), rot, rot,
                  pl.BlockSpec((None, chunk, chunk), lambda b, h, t: (h, 0, 0)),
                  pl.BlockSpec((None, chunk, LANES), lambda b, h, t: (h, 0, 0)),
                  pl.BlockSpec((None, chunk, LANES), lambda b, h, t: (h, 0, 0)),
                  pl.BlockSpec((None, 1, LANES), lambda b, h, t: (h, 0, 0)),
                  pl.BlockSpec((1, LANES), lambda b, h, t: (0, 0))],
        out_specs=pl.BlockSpec((MIX_TT, LANES), lambda b, h, t: (b * nt + t, h)),
        out_shape=jax.ShapeDtypeStruct((batch * seq, GROUP_WIDTH), BF16),
        scratch_shapes=[pltpu.VMEM((LANES, LANES), F32)],
        compiler_params=_params(("parallel", "parallel", "arbitrary")),
        name="retention",
    )(proj, proj, proj, proj, cos2, sin2, dmat, qdec, kdec, cdec, norm_w.reshape(1, LANES))


def _lru_kernel(x_ref, y_ref, cw_ref, cb_ref, wa_ref, ba_ref, wx_ref, bx_ref, lam_ref, o_ref,
                xbuf_ref, a_ref, u_ref, carry_ref, *, scan_chunk):
    tt = x_ref.shape[0]

    @pl.when(pl.program_id(2) == 0)
    def _():
        xbuf_ref[pl.ds(0, SUBLANES), :] = jnp.zeros((SUBLANES, LANES), F32)
        carry_ref[...] = jnp.zeros_like(carry_ref)

    xbuf_ref[pl.ds(SUBLANES, tt), :] = x_ref[...]
    xc = cb_ref[...]
    for j in range(CONV_WIDTH):
        xc = xc + cw_ref[pl.ds(j, 1), :] * xbuf_ref[pl.ds(SUBLANES - (CONV_WIDTH - 1) + j, tt), :]
    xbuf_ref[pl.ds(0, SUBLANES), :] = x_ref[pl.ds(tt - SUBLANES, SUBLANES), :]

    xcb = xc.astype(BF16)
    r = _sigmoid(_dot(xcb, wa_ref[...].astype(BF16)) + ba_ref[...])
    i = _sigmoid(_dot(xcb, wx_ref[...].astype(BF16)) + bx_ref[...])
    log_a = (-LRU_C) * r * _softplus(-lam_ref[...])
    a_ref[...] = jnp.exp(log_a)
    th = jnp.tanh(log_a)
    u_ref[...] = jnp.sqrt(-2.0 * th / (1.0 - th)) * (i * xc)

    row = lax.broadcasted_iota(jnp.int32, (scan_chunk, LANES), 0)
    for c in range(tt // scan_chunk):
        rows = pl.ds(c * scan_chunk, scan_chunk)
        a = a_ref[rows, :]
        u = u_ref[rows, :]
        shift = 1
        while shift < scan_chunk:
            valid = row >= shift
            u = jnp.where(valid, a * pltpu.roll(u, shift, 0) + u, u)
            a = jnp.where(valid, a * pltpu.roll(a, shift, 0), a)
            shift *= 2
        h = a * carry_ref[...] + u
        carry_ref[...] = h[scan_chunk - 1:scan_chunk, :]
        o_ref[rows, :] = (h * jax.nn.gelu(y_ref[rows, :])).astype(o_ref.dtype)


def _rg_lru(proj, conv_w, conv_b, wa, ba, wx, bx, lam, *, batch, seq):
    nt = seq // MIX_TT
    vec = pl.BlockSpec((1, LANES), lambda b, n, t: (0, n))
    mat = pl.BlockSpec((None, LANES, LANES), lambda b, n, t: (n, 0, 0))
    return pl.pallas_call(
        functools.partial(_lru_kernel, scan_chunk=SCAN_CHUNK),
        grid=(batch, LRU_BLOCKS, nt),
        in_specs=[pl.BlockSpec((MIX_TT, LANES), lambda b, n, t: (b * nt + t, CB_B_X + n)),
                  pl.BlockSpec((MIX_TT, LANES), lambda b, n, t: (b * nt + t, CB_B_Y + n)),
                  pl.BlockSpec((CONV_WIDTH, LANES), lambda b, n, t: (0, n)),
                  vec, mat, vec, mat, vec, vec],
        out_specs=pl.BlockSpec((MIX_TT, LANES), lambda b, n, t: (b * nt + t, n)),
        out_shape=jax.ShapeDtypeStruct((batch * seq, GROUP_WIDTH), BF16),
        scratch_shapes=[pltpu.VMEM((MIX_TT + SUBLANES, LANES), F32),
                        pltpu.VMEM((MIX_TT, LANES), F32),
                        pltpu.VMEM((MIX_TT, LANES), F32),
                        pltpu.VMEM((1, LANES), F32)],
        compiler_params=_params(("parallel", "parallel", "arbitrary")),
        name="rg_lru",
    )(proj, proj, conv_w, conv_b.reshape(1, -1), wa, ba.reshape(1, -1), wx, bx.reshape(1, -1),
      lam.reshape(1, -1))


def _moe_ffn_ln(h, hp, sorted_buf, router_w, exp_w_gate, exp_w_up, exp_w_down, lead, ln_g, ln_b, *, alpha):
    m, d = h.shape
    rw = jnp.concatenate([router_w, jnp.zeros((d, LANES - N_EXPERTS), F32)], axis=1)
    ei, wt, rk, cnt = _router(h, rw, tm=256)

    counts = cnt[0, :N_EXPERTS]
    padded = ((counts + MOE_TR - 1) // MOE_TR) * MOE_TR
    ends = jnp.cumsum(padded)
    offs = ends - padded
    dst1 = offs[ei[:, 0]] + rk[:, 0]
    dst2 = offs[ei[:, 1]] + rk[:, 1]
    starts = jnp.arange(sorted_buf.shape[0] // MOE_TR, dtype=jnp.int32) * MOE_TR
    te = jnp.minimum(jnp.sum(starts[:, None] >= ends[None, :], axis=1), N_EXPERTS - 1).astype(jnp.int32)
    tv = (starts < ends[N_EXPERTS - 1]).astype(jnp.int32)
    prev = jnp.concatenate([jnp.full((1,), -1, jnp.int32), te[:-1]])
    tf = (te != prev).astype(jnp.int32) * tv
    n_tiles = starts.shape[0]
    tg = (jnp.cumsum(tf) - 1).astype(jnp.int32)
    ng = jnp.sum(tf).astype(jnp.int32).reshape(1)
    group_start = jnp.where(tf == 1, jnp.arange(n_tiles, dtype=jnp.int32), n_tiles)
    next_start = lax.cummin(group_start, axis=0, reverse=True)
    next_start = jnp.concatenate([next_start[1:], jnp.full((1,), n_tiles, jnp.int32)])
    tlast = (next_start >= n_tiles).astype(jnp.int32)
    tnx = jnp.where(tlast == 1, te[0], te[jnp.minimum(next_start, n_tiles - 1)]).astype(jnp.int32)
    sched = (te, tf, tv, tnx, tlast, tg, ng)

    sorted_buf = _dispatch(hp, dst1, dst2, sorted_buf, td=256)
    hid = _moe_up(sorted_buf, exp_w_gate, exp_w_up, lead, sched, tn=512)
    y = _moe_down(hid, exp_w_down, lead, sched, tn=2048)
    h, hb = _combine_ln(y, dst1, dst2, h, wt, ln_g, ln_b, alpha=alpha, tc=256)
    return h, hb, sorted_buf


def kernel(x, w_in, w_out, hgrn_lb, hgrn_norm, conv_w, conv_b, lru_wa, lru_ba, lru_wx, lru_bx, lru_lambda,
           gla_w_gate, gla_b_gate, gla_norm, ret_norm, ln1_g, ln1_b, ln2_g, ln2_b, ffn_w_gate, ffn_w_up,
           ffn_w_down, router_w, exp_w_gate, exp_w_up, exp_w_down):
    batch, seq, d = x.shape
    depth = w_in.shape[0]
    m = batch * seq
    alpha = (2 * depth) ** 0.25
    tables = _retention_tables(seq, MIX_CHUNK)

    sorted_rows = TOP_K * m + N_EXPERTS * MOE_TR
    sorted_buf = jnp.zeros((sorted_rows, d // 2), jnp.uint32)

    w_in_t = jnp.swapaxes(w_in, 1, 2).astype(BF16)
    w_r = jnp.pad(w_in_t[:, ABC_COLS:D_COL0, :], ((0, 0), (0, LANES - GLA_RANK), (0, 0)))
    w_down = ffn_w_down.astype(BF16)

    h = x.reshape(m, d)
    hb = h.astype(BF16)
    for layer in range(depth):
        proj = _matmul_nt(hb, w_in_t, layer, row0=0, n=ABC_COLS, tm=1024, tn=512, out_dtype=F32, name="in_proj")
        proj_d = _matmul_nt(hb, w_in_t, layer, row0=D_COL0, n=d, tm=1024, tn=512, out_dtype=F32,
                            name="in_proj_ret")
        proj_r = _matmul_nt(hb, w_r, layer, row0=0, n=LANES, tm=1024, tn=LANES, out_dtype=F32,
                            name="in_proj_gate")

        w_gate_p = jnp.concatenate(
            [gla_w_gate[layer], jnp.zeros((LANES - GLA_RANK, gla_w_gate.shape[2]), F32)], axis=0)
        o_a = _hgrn(proj, hgrn_lb, hgrn_norm[layer], layer=layer, batch=batch, seq=seq)
        o_b = _rg_lru(proj, conv_w[layer], conv_b[layer], lru_wa[layer], lru_ba[layer], lru_wx[layer],
                      lru_bx[layer], lru_lambda[layer], batch=batch, seq=seq)
        o_c = _gla(proj, proj_r, w_gate_p, gla_b_gate[layer], gla_norm[layer], batch=batch, seq=seq)
        o_d = _retention(proj_d, tables, ret_norm[layer], batch=batch, seq=seq)
        mix = _matmul_wres([o_a, o_b, o_c, o_d], w_out, layer, n_cols=d, tm=1024, tn=512, out_dtype=BF16,
                           name="out_proj")

        j = layer // 2
        if layer % 2 == 0:
            h, hb = _add_ln(h, mix, ln1_g[layer], ln1_b[layer], alpha=alpha, tm=256)
            hid = _swiglu_up(hb, ffn_w_gate, ffn_w_up, j, tm=1024, tn=256)
            ff = _matmul(hid, w_down, j, tm=512, tn=1024, tk=4096, out_dtype=BF16, name="ffn_down")
            h, hb = _add_ln(h, ff, ln2_g[layer], ln2_b[layer], alpha=alpha, tm=256)
        else:
            h, hb, hp = _add_ln(h, mix, ln1_g[layer], ln1_b[layer], alpha=alpha, tm=256, packed=True)
            h, hb, sorted_buf = _moe_ffn_ln(h, hp, sorted_buf, router_w[j], exp_w_gate, exp_w_up, exp_w_down, j,
                                            ln2_g[layer], ln2_b[layer], alpha=alpha)
    return h.reshape(batch, seq, d)
```

```python
import functools
import math

import jax
import jax.numpy as jnp
import numpy as np
from jax import lax
from jax.experimental import pallas as pl
from jax.experimental.pallas import tpu as pltpu

F32 = jnp.float32
BF16 = jnp.bfloat16

LANES = 128
SUBLANES = 8
VMEM_LIMIT = 56 * 1024 * 1024

D_MODEL = 4096
N_GROUPS = 4
GROUP_WIDTH = D_MODEL // N_GROUPS
HGRN_HEADS = 8
LRU_BLOCKS = 8
CONV_WIDTH = 4
LRU_C = 8.0
GLA_HEADS = 4
GLA_KEY = 128
GLA_VAL = 256
GLA_RANK = 16
GLA_NORMALIZER = 16.0
RET_HEADS = 8
ROPE_BASE = 10000.0
N_EXPERTS = 8
LN_EPS = 1e-5
RMS_EPS = 1e-6

CB_A_Q, CB_A_F, CB_A_I, CB_A_G = 0, 8, 16, 24
CB_B_X, CB_B_Y = 32, 40
CB_C_Q, CB_C_K, CB_C_V, CB_C_G = 48, 52, 56, 64
CB_D_Q, CB_D_K, CB_D_V, CB_D_G = 0, 8, 16, 24
ABC_COLS = 9216
D_COL0 = ABC_COLS + GLA_RANK

HGRN_CHUNK = 128
GLA_CHUNK = 256
RET_CHUNK = 256
MIX_TT = 1024
SCAN_CHUNK = 64
MOE_TR = 256
TOP_K = 2


def _params(semantics):
    return pltpu.CompilerParams(dimension_semantics=semantics, vmem_limit_bytes=VMEM_LIMIT)


def _sigmoid(x):
    return 1.0 / (1.0 + jnp.exp(-x))


def _sigmoid_tanh(x):
    return 0.5 * jnp.tanh(0.5 * x) + 0.5


def _silu(x):
    return x * _sigmoid_tanh(x)


def _softplus(x):
    return jnp.maximum(x, 0.0) + jnp.log1p(jnp.exp(-jnp.abs(x)))


def _dot(a, b):
    return jnp.dot(a, b, preferred_element_type=F32)


def _dot_nt(a, b):
    return lax.dot_general(a, b, (((1,), (1,)), ((), ())), preferred_element_type=F32)


def _dot_tn(a, b):
    return lax.dot_general(a, b, (((0,), (0,)), ((), ())), preferred_element_type=F32)


def _matmul_kernel(a_ref, b_ref, o_ref, *scratch, nk):
    if nk == 1:
        o_ref[...] = _dot(a_ref[...], b_ref[...]).astype(o_ref.dtype)
        return
    acc_ref, = scratch
    k = pl.program_id(2)

    @pl.when(k == 0)
    def _():
        acc_ref[...] = jnp.zeros_like(acc_ref)

    acc_ref[...] += _dot(a_ref[...], b_ref[...])

    @pl.when(k == nk - 1)
    def _():
        o_ref[...] = acc_ref[...].astype(o_ref.dtype)


def _matmul_nt_kernel(a_ref, bt_ref, o_ref, *, sub):
    a = a_ref[...]
    for c in range(bt_ref.shape[1] // sub):
        w = bt_ref[0, pl.ds(c * sub, sub), :].astype(BF16)
        o_ref[:, pl.ds(c * sub, sub)] = _dot_nt(a, w).astype(o_ref.dtype)


def _matmul_nt(a, bt, lead, *, row0, n, tm, tn, out_dtype, name):
    m, kdim = a.shape
    assert m % tm == 0 and n % tn == 0
    return pl.pallas_call(
        functools.partial(_matmul_nt_kernel, sub=min(tn, 256)),
        grid=(m // tm, n // tn),
        in_specs=[pl.BlockSpec((tm, kdim), lambda i, j: (i, 0)),
                  pl.BlockSpec((pl.Element(1), pl.Element(tn), pl.Element(kdim)),
                               lambda i, j: (lead, pl.multiple_of(row0 + j * tn, 16), 0))],
        out_specs=pl.BlockSpec((tm, tn), lambda i, j: (i, j)),
        out_shape=jax.ShapeDtypeStruct((m, n), out_dtype),
        compiler_params=_params(("parallel", "parallel")),
        name=name,
    )(a, bt)


def _matmul(a, b, lead, *, tm, tn, tk=None, out_dtype, name):
    m, kdim = a.shape
    n = b.shape[2]
    tk = kdim if tk is None else tk
    assert m % tm == 0 and n % tn == 0 and kdim % tk == 0
    nk = kdim // tk
    return pl.pallas_call(
        functools.partial(_matmul_kernel, nk=nk),
        grid=(m // tm, n // tn, nk),
        in_specs=[pl.BlockSpec((tm, tk), lambda i, j, k: (i, k)),
                  pl.BlockSpec((None, tk, tn), lambda i, j, k: (lead, k, j))],
        out_specs=pl.BlockSpec((tm, tn), lambda i, j, k: (i, j)),
        out_shape=jax.ShapeDtypeStruct((m, n), out_dtype),
        scratch_shapes=[pltpu.VMEM((tm, tn), F32)] if nk > 1 else [],
        compiler_params=_params(("parallel", "parallel", "arbitrary")),
        name=name,
    )(a, b)


def _swiglu_up_kernel(x_ref, wg_ref, wu_ref, o_ref, wgb_ref, wub_ref):
    @pl.when(pl.program_id(1) == 0)
    def _():
        wgb_ref[...] = wg_ref[...].astype(BF16)
        wub_ref[...] = wu_ref[...].astype(BF16)

    x = x_ref[...]
    g = _dot(x, wgb_ref[...])
    u = _dot(x, wub_ref[...])
    o_ref[...] = (_silu(g) * u).astype(o_ref.dtype)


def _swiglu_up(x, wg, wu, lead, *, tm, tn):
    m, d = x.shape
    f = wg.shape[2]
    wspec = pl.BlockSpec((None, d, tn), lambda j, i: (lead, 0, j))
    return pl.pallas_call(
        _swiglu_up_kernel,
        grid=(f // tn, m // tm),
        in_specs=[pl.BlockSpec((tm, d), lambda j, i: (i, 0)), wspec, wspec],
        out_specs=pl.BlockSpec((tm, tn), lambda j, i: (i, j)),
        out_shape=jax.ShapeDtypeStruct((m, f), BF16),
        scratch_shapes=[pltpu.VMEM((d, tn), BF16), pltpu.VMEM((d, tn), BF16)],
        compiler_params=_params(("parallel", "arbitrary")),
        name="swiglu_up",
    )(x, wg, wu)


def _wres_kernel(*refs, k_sizes):
    n_a = len(k_sizes)
    a_refs = refs[:n_a]
    b_ref, o_ref, wb_ref = refs[n_a:]

    @pl.when(pl.program_id(1) == 0)
    def _():
        wb_ref[...] = b_ref[...].astype(BF16)

    acc = None
    off = 0
    for a_ref, kg in zip(a_refs, k_sizes):
        part = _dot(a_ref[...], wb_ref[pl.ds(off, kg), :])
        acc = part if acc is None else acc + part
        off += kg
    o_ref[...] = acc.astype(o_ref.dtype)


def _matmul_wres(a_list, b, lead, *, n_cols, tm, tn, out_dtype, name):
    m = a_list[0].shape[0]
    k_sizes = tuple(a.shape[1] for a in a_list)
    kdim = sum(k_sizes)
    if lead is None:
        bspec = pl.BlockSpec((kdim, tn), lambda j, i: (0, j))
    else:
        bspec = pl.BlockSpec((None, kdim, tn), lambda j, i: (lead, 0, j))
    return pl.pallas_call(
        functools.partial(_wres_kernel, k_sizes=k_sizes),
        grid=(n_cols // tn, m // tm),
        in_specs=[pl.BlockSpec((tm, kg), lambda j, i: (i, 0)) for kg in k_sizes] + [bspec],
        out_specs=pl.BlockSpec((tm, tn), lambda j, i: (i, j)),
        out_shape=jax.ShapeDtypeStruct((m, n_cols), out_dtype),
        scratch_shapes=[pltpu.VMEM((kdim, tn), BF16)],
        compiler_params=_params(("parallel", "arbitrary")),
        name=name,
    )(*a_list, b)


def _router_kernel(h_ref, w_ref, ei_ref, wt_ref, rk_ref, cnt_ref, carry_ref):
    @pl.when(pl.program_id(0) == 0)
    def _():
        carry_ref[...] = jnp.zeros_like(carry_ref)

    h = h_ref[...]
    w = w_ref[...]
    h_hi = h.astype(BF16)
    h_mid = (h - h_hi.astype(F32)).astype(BF16)
    w_hi = w.astype(BF16)
    w_mid = (w - w_hi.astype(F32)).astype(BF16)
    logits = _dot(h_hi, w_hi) + _dot(h_hi, w_mid) + _dot(h_mid, w_hi)
    tm = logits.shape[0]
    lane = lax.broadcasted_iota(jnp.int32, logits.shape, 1)
    lanef = lane.astype(F32)
    neg = jnp.float32(-jnp.inf)
    logits = jnp.where(lane < N_EXPERTS, logits, neg)
    m1 = jnp.max(logits, axis=1, keepdims=True)
    i1 = jnp.min(jnp.where(logits == m1, lanef, float(LANES)), axis=1, keepdims=True)
    rest = jnp.where(lanef == i1, neg, logits)
    m2 = jnp.max(rest, axis=1, keepdims=True)
    i2 = jnp.min(jnp.where(rest == m2, lanef, float(LANES)), axis=1, keepdims=True)
    e2 = jnp.exp(m2 - m1)
    w1 = 1.0 / (1.0 + e2)
    w2 = e2 / (1.0 + e2)

    sel1 = lanef == i1
    sel2 = lanef == i2
    onehot = jnp.where(sel1 | sel2, 1.0, 0.0)
    t = lax.broadcasted_iota(jnp.int32, (tm, tm), 0)
    s = lax.broadcasted_iota(jnp.int32, (tm, tm), 1)
    before = jnp.where(t > s, 1.0, 0.0).astype(BF16)
    seen = _dot(before, onehot.astype(BF16)) + carry_ref[...]
    r1 = jnp.sum(jnp.where(sel1, seen, 0.0), axis=1, keepdims=True)
    r2 = jnp.sum(jnp.where(sel2, seen, 0.0), axis=1, keepdims=True)
    total = carry_ref[...] + jnp.sum(onehot, axis=0, keepdims=True)
    carry_ref[...] = total

    ei_ref[...] = jnp.where(lane == 0, i1, jnp.where(lane == 1, i2, 0.0)).astype(jnp.int32)
    wt_ref[...] = jnp.where(lane == 0, w1, jnp.where(lane == 1, w2, 0.0))
    rk_ref[...] = jnp.where(lane == 0, r1, jnp.where(lane == 1, r2, 0.0)).astype(jnp.int32)
    cnt_ref[...] = jnp.broadcast_to(total, cnt_ref.shape).astype(jnp.int32)


def _router(h, router_w_padded, *, tm):
    m, d = h.shape
    tok = pl.BlockSpec((tm, LANES), lambda i: (i, 0))
    return pl.pallas_call(
        _router_kernel,
        grid=(m // tm,),
        in_specs=[pl.BlockSpec((tm, d), lambda i: (i, 0)),
                  pl.BlockSpec((d, LANES), lambda i: (0, 0))],
        out_specs=[tok, tok, tok, pl.BlockSpec((SUBLANES, LANES), lambda i: (0, 0))],
        out_shape=[jax.ShapeDtypeStruct((m, LANES), jnp.int32), jax.ShapeDtypeStruct((m, LANES), F32),
                   jax.ShapeDtypeStruct((m, LANES), jnp.int32),
                   jax.ShapeDtypeStruct((SUBLANES, LANES), jnp.int32)],
        scratch_shapes=[pltpu.VMEM((1, LANES), F32)],
        compiler_params=_params(("arbitrary",)),
        name="router",
    )(h, router_w_padded)


def _dispatch_kernel(d1_ref, d2_ref, x_ref, buf_in_ref, o_ref, sem, *, td):
    del buf_in_ref
    base = pl.program_id(0) * td

    def row_copy(src_row, dst_row):
        return pltpu.make_async_copy(x_ref.at[pl.ds(src_row, 1), :], o_ref.at[pl.ds(dst_row, 1), :], sem)

    def issue(t, carry):
        row_copy(t, d1_ref[base + t]).start()
        row_copy(t, d2_ref[base + t]).start()
        return carry

    lax.fori_loop(0, td, issue, 0, unroll=8)

    def drain(t, carry):
        row_copy(0, 0).wait()
        row_copy(0, 0).wait()
        return carry

    lax.fori_loop(0, td, drain, 0, unroll=8)


def _dispatch(xp, dst1, dst2, buf, *, td):
    m = xp.shape[0]
    return pl.pallas_call(
        functools.partial(_dispatch_kernel, td=td),
        grid_spec=pltpu.PrefetchScalarGridSpec(
            num_scalar_prefetch=2, grid=(m // td,),
            in_specs=[pl.BlockSpec((td, xp.shape[1]), lambda i, d1, d2: (i, 0)),
                      pl.BlockSpec(memory_space=pl.ANY)],
            out_specs=pl.BlockSpec(memory_space=pl.ANY),
            scratch_shapes=[pltpu.SemaphoreType.DMA(())]),
        out_shape=jax.ShapeDtypeStruct(buf.shape, buf.dtype),
        input_output_aliases={3: 0},
        compiler_params=_params(("arbitrary",)),
        name="moe_dispatch",
    )(dst1, dst2, xp, buf)


def _unpack_halves(xp):
    lo = lax.bitcast_convert_type(xp << jnp.uint32(16), F32).astype(BF16)
    hi = lax.bitcast_convert_type(xp & jnp.uint32(0xFFFF0000), F32).astype(BF16)
    return lo, hi


def _load_group_weights(j, i, sched, w_hbm_refs, wbuf, wb_refs, sems, *, lead, tn, nj):
    te_ref, tf_ref, _, tnx_ref, tlast_ref, tg_ref, ng_ref = sched

    def fetch(e, jj, slot):
        col = pl.multiple_of(jj * tn, tn)
        for m, w in enumerate(w_hbm_refs):
            pltpu.make_async_copy(w.at[lead, e, :, pl.ds(col, tn)], wbuf.at[slot, m], sems.at[slot]).start()

    @pl.when(tf_ref[i] == 1)
    def _():
        g = j * ng_ref[0] + tg_ref[i]
        slot = g % 2

        @pl.when(g == 0)
        def _():
            fetch(te_ref[i], j, slot)

        last = tlast_ref[i] == 1

        @pl.when(jnp.logical_not(last))
        def _():
            fetch(tnx_ref[i], j, 1 - slot)

        @pl.when(jnp.logical_and(last, j + 1 < nj))
        def _():
            fetch(tnx_ref[i], j + 1, 1 - slot)

        for m, w in enumerate(w_hbm_refs):
            pltpu.make_async_copy(w.at[lead, 0, :, pl.ds(0, tn)], wbuf.at[slot, m], sems.at[slot]).wait()
        for m, wb in enumerate(wb_refs):
            wb[...] = wbuf[slot, m].astype(BF16)


def _moe_up_kernel(*refs, lead, tn, nj):
    sched = refs[:7]
    x_ref, wg_hbm, wu_hbm, o_ref, wbuf, wgb_ref, wub_ref, sems = refs[7:]
    tv_ref = sched[2]
    j = pl.program_id(0)
    i = pl.program_id(1)
    _load_group_weights(j, i, sched, (wg_hbm, wu_hbm), wbuf, (wgb_ref, wub_ref), sems, lead=lead, tn=tn, nj=nj)

    @pl.when(tv_ref[i] == 1)
    def _():
        half = x_ref.shape[1]
        lo, hi = _unpack_halves(x_ref[...])
        g = _dot(lo, wgb_ref[pl.ds(0, half), :]) + _dot(hi, wgb_ref[pl.ds(half, half), :])
        u = _dot(lo, wub_ref[pl.ds(0, half), :]) + _dot(hi, wub_ref[pl.ds(half, half), :])
        o_ref[...] = (_silu(g) * u).astype(o_ref.dtype)

    @pl.when(tv_ref[i] == 0)
    def _():
        o_ref[...] = jnp.zeros_like(o_ref)


def _moe_up(xs, wg, wu, lead, sched, *, tn):
    rows, half = xs.shape
    d = 2 * half
    f = wg.shape[3]
    nj = f // tn
    hbm = pl.BlockSpec(memory_space=pl.ANY)
    return pl.pallas_call(
        functools.partial(_moe_up_kernel, lead=lead, tn=tn, nj=nj),
        grid_spec=pltpu.PrefetchScalarGridSpec(
            num_scalar_prefetch=len(sched), grid=(nj, rows // MOE_TR),
            in_specs=[pl.BlockSpec((MOE_TR, half), lambda j, i, *_: (i, 0)), hbm, hbm],
            out_specs=pl.BlockSpec((MOE_TR, tn), lambda j, i, *_: (i, j)),
            scratch_shapes=[pltpu.VMEM((2, 2, d, tn), F32), pltpu.VMEM((d, tn), BF16), pltpu.VMEM((d, tn), BF16),
                            pltpu.SemaphoreType.DMA((2,))]),
        out_shape=jax.ShapeDtypeStruct((rows, f), BF16),
        compiler_params=_params(("arbitrary", "arbitrary")),
        name="moe_up",
    )(*sched, xs, wg, wu)


def _moe_down_kernel(*refs, lead, tn, nj):
    sched = refs[:7]
    x_ref, w_hbm, o_ref, wbuf, wb_ref, sems = refs[7:]
    tv_ref = sched[2]
    j = pl.program_id(0)
    i = pl.program_id(1)
    _load_group_weights(j, i, sched, (w_hbm,), wbuf, (wb_ref,), sems, lead=lead, tn=tn, nj=nj)

    @pl.when(tv_ref[i] == 1)
    def _():
        o_ref[...] = _dot(x_ref[...], wb_ref[...])

    @pl.when(tv_ref[i] == 0)
    def _():
        o_ref[...] = jnp.zeros_like(o_ref)


def _moe_down(hid, wd, lead, sched, *, tn):
    rows, f = hid.shape
    d = wd.shape[3]
    nj = d // tn
    return pl.pallas_call(
        functools.partial(_moe_down_kernel, lead=lead, tn=tn, nj=nj),
        grid_spec=pltpu.PrefetchScalarGridSpec(
            num_scalar_prefetch=len(sched), grid=(nj, rows // MOE_TR),
            in_specs=[pl.BlockSpec((MOE_TR, f), lambda j, i, *_: (i, 0)), pl.BlockSpec(memory_space=pl.ANY)],
            out_specs=pl.BlockSpec((MOE_TR, tn), lambda j, i, *_: (i, j)),
            scratch_shapes=[pltpu.VMEM((2, 1, f, tn), F32), pltpu.VMEM((f, tn), BF16),
                            pltpu.SemaphoreType.DMA((2,))]),
        out_shape=jax.ShapeDtypeStruct((rows, d), F32),
        compiler_params=_params(("arbitrary", "arbitrary")),
        name="moe_down",
    )(*sched, hid, wd)


def _combine_ln_kernel(d1_ref, d2_ref, y_ref, h_ref, wt_ref, g_ref, b_ref, o_ref, ob_ref,
                       buf1, buf2, sems, *, alpha, tc):
    i = pl.program_id(0)
    slot = i % 2

    def row_copy(src_row, buf, s, t):
        return pltpu.make_async_copy(y_ref.at[pl.ds(src_row, 1), :], buf.at[s, pl.ds(t, 1), :], sems.at[s])

    def gather_tile(step, s):
        base = step * tc

        def issue(t, carry):
            row_copy(d1_ref[base + t], buf1, s, t).start()
            row_copy(d2_ref[base + t], buf2, s, t).start()
            return carry

        lax.fori_loop(0, tc, issue, 0, unroll=8)

    @pl.when(i == 0)
    def _():
        gather_tile(0, 0)

    @pl.when(i + 1 < pl.num_programs(0))
    def _():
        gather_tile(i + 1, 1 - slot)

    def drain(t, carry):
        row_copy(0, buf1, slot, 0).wait()
        row_copy(0, buf2, slot, 0).wait()
        return carry

    lax.fori_loop(0, tc, drain, 0, unroll=8)

    wt = wt_ref[...]
    x = alpha * h_ref[...] + wt[:, 0:1] * buf1[slot] + wt[:, 1:2] * buf2[slot]
    mu = jnp.mean(x, axis=-1, keepdims=True)
    xc = x - mu
    var = jnp.mean(xc * xc, axis=-1, keepdims=True)
    out = xc * lax.rsqrt(var + LN_EPS) * g_ref[...] + b_ref[...]
    o_ref[...] = out
    ob_ref[...] = out.astype(BF16)


def _combine_ln(y, dst1, dst2, h, wt, g, b, *, alpha, tc):
    m, d = h.shape
    row = pl.BlockSpec((tc, d), lambda i, d1, d2: (i, 0))
    vec = pl.BlockSpec((1, d), lambda i, d1, d2: (0, 0))
    return pl.pallas_call(
        functools.partial(_combine_ln_kernel, alpha=alpha, tc=tc),
        grid_spec=pltpu.PrefetchScalarGridSpec(
            num_scalar_prefetch=2, grid=(m // tc,),
            in_specs=[pl.BlockSpec(memory_space=pl.ANY), row,
                      pl.BlockSpec((tc, LANES), lambda i, d1, d2: (i, 0)), vec, vec],
            out_specs=[row, row],
            scratch_shapes=[pltpu.VMEM((2, tc, d), F32), pltpu.VMEM((2, tc, d), F32),
                            pltpu.SemaphoreType.DMA((2,))]),
        out_shape=[jax.ShapeDtypeStruct((m, d), F32), jax.ShapeDtypeStruct((m, d), BF16)],
        compiler_params=_params(("arbitrary",)),
        name="moe_combine_ln",
    )(dst1, dst2, y, h, wt, g.reshape(1, d), b.reshape(1, d))


def _add_ln_kernel(h_ref, y_ref, g_ref, b_ref, o_ref, ob_ref, *maybe_packed_ref, alpha):
    x = alpha * h_ref[...] + y_ref[...]
    mu = jnp.mean(x, axis=-1, keepdims=True)
    xc = x - mu
    var = jnp.mean(xc * xc, axis=-1, keepdims=True)
    out = xc * lax.rsqrt(var + LN_EPS) * g_ref[...] + b_ref[...]
    o_ref[...] = out
    out_b = out.astype(BF16)
    ob_ref[...] = out_b
    if maybe_packed_ref:
        half = out.shape[1] // 2
        as_u32 = lax.bitcast_convert_type(out_b.astype(F32), jnp.uint32)
        maybe_packed_ref[0][...] = (as_u32[:, :half] >> jnp.uint32(16)) | as_u32[:, half:]


def _add_ln(h, y, g, b, *, alpha, tm, packed=False):
    m, d = h.shape
    row = pl.BlockSpec((tm, d), lambda i: (i, 0))
    vec = pl.BlockSpec((1, d), lambda i: (0, 0))
    out_specs = [row, row]
    out_shape = [jax.ShapeDtypeStruct((m, d), F32), jax.ShapeDtypeStruct((m, d), BF16)]
    if packed:
        out_specs.append(pl.BlockSpec((tm, d // 2), lambda i: (i, 0)))
        out_shape.append(jax.ShapeDtypeStruct((m, d // 2), jnp.uint32))
    return pl.pallas_call(
        functools.partial(_add_ln_kernel, alpha=alpha),
        grid=(m // tm,),
        in_specs=[row, row, vec, vec],
        out_specs=out_specs,
        out_shape=out_shape,
        compiler_params=_params(("parallel",)),
        name="add_ln",
    )(h, y, g.reshape(1, d), b.reshape(1, d))


def _level_masks(c):
    t = lax.broadcasted_iota(jnp.int32, (c, c), 0)
    s = lax.broadcasted_iota(jnp.int32, (c, c), 1)
    x = t ^ s
    masks = [t == s]
    for p in range(1, int(math.log2(c)) + 1):
        masks.append((t > s) & ((x >> (p - 1)) == 1))
    return masks


def _midpoint_rows(cum_ref, c, k, p):
    blk = 1 << p
    half = blk >> 1
    if blk >= SUBLANES:
        parts = [jnp.broadcast_to(cum_ref[pl.ds(b * blk + half - 1, 1), :], (blk, k))
                 for b in range(c // blk)]
        return parts[0] if len(parts) == 1 else jnp.concatenate(parts, axis=0)
    sub = lax.broadcasted_iota(jnp.int32, (c, k), 0) & (SUBLANES - 1)
    res = None
    for j in range(SUBLANES // blk):
        rows = jnp.concatenate(
            [jnp.broadcast_to(cum_ref[pl.ds(SUBLANES * v + j * blk + half - 1, 1), :], (SUBLANES, k))
             for v in range(c // SUBLANES)], axis=0)
        res = rows if res is None else jnp.where(sub >= j * blk, rows, res)
    return res


def _split3(x):
    hi = x.astype(BF16)
    r1 = x - hi.astype(F32)
    mid = r1.astype(BF16)
    lo = (r1 - mid.astype(F32)).astype(BF16)
    return hi, mid, lo


def _gla_chunk(q, k, v, g, state_ref, cum_ref, masks, tri):
    c, kd = q.shape
    g_hi, g_mid, g_lo = _split3(g)
    cum = _dot(tri, g_hi) + _dot(tri, g_mid) + _dot(tri, g_lo)
    cum_ref[...] = cum
    row = lax.broadcasted_iota(jnp.int32, (c, kd), 0)
    qb = q.astype(BF16)
    kb = k.astype(BF16)
    vb = v.astype(BF16)
    scores = jnp.where(masks[0], _dot_nt(qb, kb), 0.0)
    for p in range(1, len(masks)):
        mid = _midpoint_rows(cum_ref, c, kd, p)
        upper = ((row >> (p - 1)) & 1) == 1
        z = (jnp.where(upper, q, k) * jnp.exp(-jnp.abs(cum - mid))).astype(BF16)
        scores = jnp.where(masks[p], _dot_nt(z, z), scores)
    state = state_ref[...]
    out = _dot(scores.astype(BF16), vb)
    out = out + _dot_nt((q * jnp.exp(cum)).astype(BF16), state.astype(BF16))
    last = cum_ref[pl.ds(c - 1, 1), :]
    k_dec = (k * jnp.exp(last - cum)).astype(BF16)
    state_ref[...] = state * jnp.exp(last) + _dot_tn(vb, k_dec)
    return out


def _tri_ones(c):
    t = lax.broadcasted_iota(jnp.int32, (c, c), 0)
    s = lax.broadcasted_iota(jnp.int32, (c, c), 1)
    return jnp.where(t >= s, 1.0, 0.0).astype(BF16)


def _hgrn_kernel(q_ref, f_ref, i_ref, g_ref, lb_ref, nw_ref, o_ref, state_ref, cum_ref, *, layer, chunk):
    @pl.when(pl.program_id(2) == 0)
    def _():
        state_ref[...] = jnp.zeros_like(state_ref)

    lb_all = lb_ref[...]
    e = jnp.exp(lb_all - jnp.max(lb_all, axis=0, keepdims=True))
    soft = e / jnp.sum(e, axis=0, keepdims=True)
    lower = jnp.zeros((1, LANES), F32)
    for l in range(1, layer + 1):
        lower = lower + soft[l:l + 1, :]
    masks = _level_masks(chunk)
    tri = _tri_ones(chunk)
    kd = q_ref.shape[1]
    for c in range(q_ref.shape[0] // chunk):
        rows = pl.ds(c * chunk, chunk)
        f = lower + (1.0 - lower) * _sigmoid(f_ref[rows, :])
        q = _silu(q_ref[rows, :]) * (kd ** -0.5)
        o = _gla_chunk(q, 1.0 - f, i_ref[rows, :], jnp.log(f), state_ref, cum_ref, masks, tri)
        o = o * lax.rsqrt(jnp.mean(o * o, axis=-1, keepdims=True) + RMS_EPS) * nw_ref[...]
        o_ref[rows, :] = (o * _silu(g_ref[rows, :])).astype(o_ref.dtype)


def _hgrn(proj, hgrn_lb, norm_w, *, layer, batch, seq):
    nt = seq // MIX_TT

    def col(cb):
        return pl.BlockSpec((MIX_TT, LANES), lambda b, h, t: (b * nt + t, cb + h))

    return pl.pallas_call(
        functools.partial(_hgrn_kernel, layer=layer, chunk=HGRN_CHUNK),
        grid=(batch, HGRN_HEADS, nt),
        in_specs=[col(CB_A_Q), col(CB_A_F), col(CB_A_I), col(CB_A_G),
                  pl.BlockSpec((hgrn_lb.shape[0], LANES), lambda b, h, t: (0, h)),
                  pl.BlockSpec((1, LANES), lambda b, h, t: (0, 0))],
        out_specs=pl.BlockSpec((MIX_TT, LANES), lambda b, h, t: (b * nt + t, h)),
        out_shape=jax.ShapeDtypeStruct((batch * seq, GROUP_WIDTH), BF16),
        scratch_shapes=[pltpu.VMEM((LANES, LANES), F32), pltpu.VMEM((HGRN_CHUNK, LANES), F32)],
        compiler_params=_params(("parallel", "parallel", "arbitrary")),
        name="hgrn",
    )(proj, proj, proj, proj, hgrn_lb, norm_w.reshape(1, LANES))


def _gla_kernel(q_ref, k_ref, v_ref, g_ref, r_ref, wg_ref, bg_ref, nw_ref, o_ref, state_ref, cum_ref, *, chunk):
    @pl.when(pl.program_id(2) == 0)
    def _():
        state_ref[...] = jnp.zeros_like(state_ref)

    masks = _level_masks(chunk)
    tri = _tri_ones(chunk)
    kd = q_ref.shape[1]
    for c in range(q_ref.shape[0] // chunk):
        rows = pl.ds(c * chunk, chunk)
        pre = _dot(r_ref[rows, :], wg_ref[...]) + bg_ref[...]
        log_alpha = (jnp.minimum(pre, 0.0) - jnp.log1p(jnp.exp(-jnp.abs(pre)))) * (1.0 / GLA_NORMALIZER)
        o = _gla_chunk(q_ref[rows, :] * (kd ** -0.5), k_ref[rows, :], v_ref[rows, :], log_alpha,
                       state_ref, cum_ref, masks, tri)
        o = o * lax.rsqrt(jnp.mean(o * o, axis=-1, keepdims=True) + RMS_EPS) * nw_ref[...]
        o_ref[rows, :] = (o * _silu(g_ref[rows, :])).astype(o_ref.dtype)


def _gla(proj, proj_r, w_gate_padded, b_gate, norm_w, *, batch, seq):
    nt = seq // MIX_TT
    vb = GLA_VAL // LANES

    return pl.pallas_call(
        functools.partial(_gla_kernel, chunk=GLA_CHUNK),
        grid=(batch, GLA_HEADS, nt),
        in_specs=[pl.BlockSpec((MIX_TT, GLA_KEY), lambda b, h, t: (b * nt + t, CB_C_Q + h)),
                  pl.BlockSpec((MIX_TT, GLA_KEY), lambda b, h, t: (b * nt + t, CB_C_K + h)),
                  pl.BlockSpec((MIX_TT, GLA_VAL), lambda b, h, t: (b * nt + t, CB_C_V // vb + h)),
                  pl.BlockSpec((MIX_TT, GLA_VAL), lambda b, h, t: (b * nt + t, CB_C_G // vb + h)),
                  pl.BlockSpec((MIX_TT, LANES), lambda b, h, t: (b * nt + t, 0)),
                  pl.BlockSpec((LANES, GLA_KEY), lambda b, h, t: (0, h)),
                  pl.BlockSpec((1, GLA_KEY), lambda b, h, t: (0, h)),
                  pl.BlockSpec((1, GLA_VAL), lambda b, h, t: (0, 0))],
        out_specs=pl.BlockSpec((MIX_TT, GLA_VAL), lambda b, h, t: (b * nt + t, h)),
        out_shape=jax.ShapeDtypeStruct((batch * seq, GROUP_WIDTH), BF16),
        scratch_shapes=[pltpu.VMEM((GLA_VAL, GLA_KEY), F32), pltpu.VMEM((GLA_CHUNK, GLA_KEY), F32)],
        compiler_params=_params(("parallel", "parallel", "arbitrary")),
        name="gla",
    )(proj, proj, proj, proj, proj_r, w_gate_padded, b_gate.reshape(1, -1), norm_w.reshape(1, GLA_VAL))


def _ret_kernel(q_ref, k_ref, v_ref, g_ref, cos_ref, sin_ref, dmat_ref, qdec_ref, kdec_ref, cdec_ref, nw_ref,
                o_ref, state_ref, *, chunk):
    @pl.when(pl.program_id(2) == 0)
    def _():
        state_ref[...] = jnp.zeros_like(state_ref)

    kd = q_ref.shape[1]
    half = kd // 2
    dmat = dmat_ref[...]
    qdec = qdec_ref[...]
    kdec = kdec_ref[...]
    cdec = cdec_ref[...]
    for c in range(q_ref.shape[0] // chunk):
        rows = pl.ds(c * chunk, chunk)
        cos = cos_ref[rows, :]
        sin = sin_ref[rows, :]
        q = q_ref[rows, :]
        k = k_ref[rows, :]
        q = q * cos + pltpu.roll(q, half, 1) * sin
        k = (k * cos + pltpu.roll(k, half, 1) * sin) * (kd ** -0.5)
        vb = v_ref[rows, :].astype(BF16)
        state = state_ref[...]
        scores = _dot_nt(q.astype(BF16), k.astype(BF16)) * dmat
        o = _dot(scores.astype(BF16), vb) + _dot_nt((q * qdec).astype(BF16), state.astype(BF16))
        state_ref[...] = state * cdec + _dot_tn(vb, (k * kdec).astype(BF16))
        mu = jnp.mean(o, axis=-1, keepdims=True)
        oc = o - mu
        var = jnp.mean(oc * oc, axis=-1, keepdims=True)
        o = oc * lax.rsqrt(var + LN_EPS) * nw_ref[...]
        o_ref[rows, :] = (o * _silu(g_ref[rows, :])).astype(o_ref.dtype)


def _retention_tables(seq, chunk):
    half = LANES // 2
    inv_freq = ROPE_BASE ** (-jnp.arange(half, dtype=F32) / half)
    ang = jnp.arange(seq, dtype=F32)[:, None] * inv_freq[None, :]
    cos, sin = jnp.cos(ang), jnp.sin(ang)
    cos2 = jnp.concatenate([cos, cos], axis=1)
    sin2 = jnp.concatenate([-sin, sin], axis=1)
    log_gamma = jnp.log1p(-(2.0 ** (-5.0 - jnp.arange(RET_HEADS, dtype=F32))))
    pos = jnp.arange(chunk, dtype=F32)
    rel = pos[:, None] - pos[None, :]
    dmat = jnp.where(rel >= 0, jnp.exp(log_gamma[:, None, None] * jnp.maximum(rel, 0.0)), 0.0)
    qdec = jnp.exp(log_gamma[:, None] * (pos + 1.0)[None, :])
    kdec = jnp.exp(log_gamma[:, None] * (chunk - 1.0 - pos)[None, :])
    cdec = jnp.exp(log_gamma * chunk)
    bcast = lambda a: jnp.broadcast_to(a[..., None], a.shape + (LANES,))
    return cos2, sin2, dmat, bcast(qdec), bcast(kdec), bcast(cdec[:, None])


def _retention(proj, tables, norm_w, *, batch, seq):
    nt = seq // MIX_TT
    cos2, sin2, dmat, qdec, kdec, cdec = tables
    chunk = dmat.shape[1]

    def col(cb):
        return pl.BlockSpec((MIX_TT, LANES), lambda b, h, t: (b * nt + t, cb + h))

    rot = pl.BlockSpec((MIX_TT, LANES), lambda b, h, t: (t, 0))
    return pl.pallas_call(
        functools.partial(_ret_kernel, chunk=chunk),
        grid=(batch, RET_HEADS, nt),
        in_specs=[col(CB_D_Q), col(CB_D_K), col(CB_D_V), col(CB_D_G), rot, rot,
                  pl.BlockSpec((None, chunk, chunk), lambda b, h, t: (h, 0, 0)),
                  pl.BlockSpec((None, chunk, LANES), lambda b, h, t: (h, 0, 0)),
                  pl.BlockSpec((None, chunk, LANES), lambda b, h, t: (h, 0, 0)),
                  pl.BlockSpec((None, 1, LANES), lambda b, h, t: (h, 0, 0)),
                  pl.BlockSpec((1, LANES), lambda b, h, t: (0, 0))],
        out_specs=pl.BlockSpec((MIX_TT, LANES), lambda b, h, t: (b * nt + t, h)),
        out_shape=jax.ShapeDtypeStruct((batch * seq, GROUP_WIDTH), BF16),
        scratch_shapes=[pltpu.VMEM((LANES, LANES), F32)],
        compiler_params=_params(("parallel", "parallel", "arbitrary")),
        name="retention",
    )(proj, proj, proj, proj, cos2, sin2, dmat, qdec, kdec, cdec, norm_w.reshape(1, LANES))


def _lru_kernel(x_ref, y_ref, cw_ref, cb_ref, wa_ref, ba_ref, wx_ref, bx_ref, lam_ref, o_ref,
                xbuf_ref, a_ref, u_ref, carry_ref, *, scan_chunk):
    tt = x_ref.shape[0]

    @pl.when(pl.program_id(2) == 0)
    def _():
        xbuf_ref[pl.ds(0, SUBLANES), :] = jnp.zeros((SUBLANES, LANES), F32)
        carry_ref[...] = jnp.zeros_like(carry_ref)

    xbuf_ref[pl.ds(SUBLANES, tt), :] = x_ref[...]
    xc = cb_ref[...]
    for j in range(CONV_WIDTH):
        xc = xc + cw_ref[pl.ds(j, 1), :] * xbuf_ref[pl.ds(SUBLANES - (CONV_WIDTH - 1) + j, tt), :]
    xbuf_ref[pl.ds(0, SUBLANES), :] = x_ref[pl.ds(tt - SUBLANES, SUBLANES), :]

    xcb = xc.astype(BF16)
    r = _sigmoid_tanh(_dot(xcb, wa_ref[...].astype(BF16)) + ba_ref[...])
    i = _sigmoid_tanh(_dot(xcb, wx_ref[...].astype(BF16)) + bx_ref[...])
    log_a = (-LRU_C) * r * _softplus(-lam_ref[...])
    a_ref[...] = jnp.exp(log_a)
    th = jnp.tanh(log_a)
    u_ref[...] = jnp.sqrt(-2.0 * th / (1.0 - th)) * (i * xc)

    row = lax.broadcasted_iota(jnp.int32, (scan_chunk, LANES), 0)
    for c in range(tt // scan_chunk):
        rows = pl.ds(c * scan_chunk, scan_chunk)
        a = a_ref[rows, :]
        u = u_ref[rows, :]
        shift = 1
        while shift < scan_chunk:
            valid = row >= shift
            u = jnp.where(valid, a * pltpu.roll(u, shift, 0) + u, u)
            a = jnp.where(valid, a * pltpu.roll(a, shift, 0), a)
            shift *= 2
        h = a * carry_ref[...] + u
        carry_ref[...] = h[scan_chunk - 1:scan_chunk, :]
        o_ref[rows, :] = (h * jax.nn.gelu(y_ref[rows, :])).astype(o_ref.dtype)


def _rg_lru(proj, conv_w, conv_b, wa, ba, wx, bx, lam, *, batch, seq):
    nt = seq // MIX_TT
    vec = pl.BlockSpec((1, LANES), lambda b, n, t: (0, n))
    mat = pl.BlockSpec((None, LANES, LANES), lambda b, n, t: (n, 0, 0))
    return pl.pallas_call(
        functools.partial(_lru_kernel, scan_chunk=SCAN_CHUNK),
        grid=(batch, LRU_BLOCKS, nt),
        in_specs=[pl.BlockSpec((MIX_TT, LANES), lambda b, n, t: (b * nt + t, CB_B_X + n)),
                  pl.BlockSpec((MIX_TT, LANES), lambda b, n, t: (b * nt + t, CB_B_Y + n)),
                  pl.BlockSpec((CONV_WIDTH, LANES), lambda b, n, t: (0, n)),
                  vec, mat, vec, mat, vec, vec],
        out_specs=pl.BlockSpec((MIX_TT, LANES), lambda b, n, t: (b * nt + t, n)),
        out_shape=jax.ShapeDtypeStruct((batch * seq, GROUP_WIDTH), BF16),
        scratch_shapes=[pltpu.VMEM((MIX_TT + SUBLANES, LANES), F32),
                        pltpu.VMEM((MIX_TT, LANES), F32),
                        pltpu.VMEM((MIX_TT, LANES), F32),
                        pltpu.VMEM((1, LANES), F32)],
        compiler_params=_params(("parallel", "parallel", "arbitrary")),
        name="rg_lru",
    )(proj, proj, conv_w, conv_b.reshape(1, -1), wa, ba.reshape(1, -1), wx, bx.reshape(1, -1),
      lam.reshape(1, -1))


def _moe_ffn_ln(h, hp, sorted_buf, router_w, exp_w_gate, exp_w_up, exp_w_down, lead, ln_g, ln_b, *, alpha):
    m, d = h.shape
    rw = jnp.concatenate([router_w, jnp.zeros((d, LANES - N_EXPERTS), F32)], axis=1)
    ei, wt, rk, cnt = _router(h, rw, tm=256)

    counts = cnt[0, :N_EXPERTS]
    padded = ((counts + MOE_TR - 1) // MOE_TR) * MOE_TR
    ends = jnp.cumsum(padded)
    offs = ends - padded
    dst1 = offs[ei[:, 0]] + rk[:, 0]
    dst2 = offs[ei[:, 1]] + rk[:, 1]
    starts = jnp.arange(sorted_buf.shape[0] // MOE_TR, dtype=jnp.int32) * MOE_TR
    te = jnp.minimum(jnp.sum(starts[:, None] >= ends[None, :], axis=1), N_EXPERTS - 1).astype(jnp.int32)
    tv = (starts < ends[N_EXPERTS - 1]).astype(jnp.int32)
    prev = jnp.concatenate([jnp.full((1,), -1, jnp.int32), te[:-1]])
    tf = (te != prev).astype(jnp.int32) * tv
    n_tiles = starts.shape[0]
    tg = (jnp.cumsum(tf) - 1).astype(jnp.int32)
    ng = jnp.sum(tf).astype(jnp.int32).reshape(1)
    group_start = jnp.where(tf == 1, jnp.arange(n_tiles, dtype=jnp.int32), n_tiles)
    next_start = lax.cummin(group_start, axis=0, reverse=True)
    next_start = jnp.concatenate([next_start[1:], jnp.full((1,), n_tiles, jnp.int32)])
    tlast = (next_start >= n_tiles).astype(jnp.int32)
    tnx = jnp.where(tlast == 1, te[0], te[jnp.minimum(next_start, n_tiles - 1)]).astype(jnp.int32)
    sched = (te, tf, tv, tnx, tlast, tg, ng)

    sorted_buf = _dispatch(hp, dst1, dst2, sorted_buf, td=256)
    hid = _moe_up(sorted_buf, exp_w_gate, exp_w_up, lead, sched, tn=512)
    y = _moe_down(hid, exp_w_down, lead, sched, tn=2048)
    h, hb = _combine_ln(y, dst1, dst2, h, wt, ln_g, ln_b, alpha=alpha, tc=256)
    return h, hb, sorted_buf


def kernel(x, w_in, w_out, hgrn_lb, hgrn_norm, conv_w, conv_b, lru_wa, lru_ba, lru_wx, lru_bx, lru_lambda,
           gla_w_gate, gla_b_gate, gla_norm, ret_norm, ln1_g, ln1_b, ln2_g, ln2_b, ffn_w_gate, ffn_w_up,
           ffn_w_down, router_w, exp_w_gate, exp_w_up, exp_w_down):
    batch, seq, d = x.shape
    depth = w_in.shape[0]
    m = batch * seq
    alpha = (2 * depth) ** 0.25
    tables = _retention_tables(seq, RET_CHUNK)

    sorted_rows = TOP_K * m + N_EXPERTS * MOE_TR
    sorted_buf = jnp.zeros((sorted_rows, d // 2), jnp.uint32)

    w_in_t = jnp.swapaxes(w_in, 1, 2)
    w_r = jnp.pad(w_in_t[:, ABC_COLS:D_COL0, :], ((0, 0), (0, LANES - GLA_RANK), (0, 0)))
    w_down = ffn_w_down.astype(BF16)

    h = x.reshape(m, d)
    hb = h.astype(BF16)
    for layer in range(depth):
        proj = _matmul_nt(hb, w_in_t, layer, row0=0, n=ABC_COLS, tm=1024, tn=512, out_dtype=F32, name="in_proj")
        proj_d = _matmul_nt(hb, w_in_t, layer, row0=D_COL0, n=d, tm=1024, tn=512, out_dtype=F32,
                            name="in_proj_ret")
        proj_r = _matmul_nt(hb, w_r, layer, row0=0, n=LANES, tm=1024, tn=LANES, out_dtype=F32,
                            name="in_proj_gate")

        w_gate_p = jnp.concatenate(
            [gla_w_gate[layer], jnp.zeros((LANES - GLA_RANK, gla_w_gate.shape[2]), F32)], axis=0)
        o_a = _hgrn(proj, hgrn_lb, hgrn_norm[layer], layer=layer, batch=batch, seq=seq)
        o_b = _rg_lru(proj, conv_w[layer], conv_b[layer], lru_wa[layer], lru_ba[layer], lru_wx[layer],
                      lru_bx[layer], lru_lambda[layer], batch=batch, seq=seq)
        o_c = _gla(proj, proj_r, w_gate_p, gla_b_gate[layer], gla_norm[layer], batch=batch, seq=seq)
        o_d = _retention(proj_d, tables, ret_norm[layer], batch=batch, seq=seq)
        mix = _matmul_wres([o_a, o_b, o_c, o_d], w_out, layer, n_cols=d, tm=1024, tn=512, out_dtype=BF16,
                           name="out_proj")

        j = layer // 2
        if layer % 2 == 0:
            h, hb = _add_ln(h, mix, ln1_g[layer], ln1_b[layer], alpha=alpha, tm=256)
            hid = _swiglu_up(hb, ffn_w_gate, ffn_w_up, j, tm=1024, tn=256)
            ff = _matmul(hid, w_down, j, tm=512, tn=1024, tk=4096, out_dtype=BF16, name="ffn_down")
            h, hb = _add_ln(h, ff, ln2_g[layer], ln2_b[layer], alpha=alpha, tm=256)
        else:
            h, hb, hp = _add_ln(h, mix, ln1_g[layer], ln1_b[layer], alpha=alpha, tm=256, packed=True)
            h, hb, sorted_buf = _moe_ffn_ln(h, hp, sorted_buf, router_w[j], exp_w_gate, exp_w_up, exp_w_down, j,
                                            ln2_g[layer], ln2_b[layer], alpha=alpha)
    return h.reshape(batch, seq, d)
```

```python
import functools
import math

import jax
import jax.numpy as jnp
from jax import lax
from jax.experimental import pallas as pl
from jax.experimental.pallas import tpu as pltpu

F32 = jnp.float32
BF16 = jnp.bfloat16

LANES = 128
SUBLANES = 8
VMEM_LIMIT = 56 * 1024 * 1024

D_MODEL = 4096
N_GROUPS = 4
GROUP_WIDTH = D_MODEL // N_GROUPS
HGRN_HEADS = 8
LRU_BLOCKS = 8
CONV_WIDTH = 4
LRU_C = 8.0
GLA_HEADS = 4
GLA_KEY = 128
GLA_VAL = 256
GLA_RANK = 16
GLA_NORMALIZER = 16.0
RET_HEADS = 8
ROPE_BASE = 10000.0
N_EXPERTS = 8
LN_EPS = 1e-5
RMS_EPS = 1e-6

CB_A_Q, CB_A_F, CB_A_I, CB_A_G = 0, 8, 16, 24
CB_B_X, CB_B_Y = 32, 40
CB_C_Q, CB_C_K, CB_C_V, CB_C_G = 48, 52, 56, 64
CB_D_Q, CB_D_K, CB_D_V, CB_D_G = 0, 8, 16, 24
ABC_COLS = 9216
D_COL0 = ABC_COLS + GLA_RANK

HGRN_CHUNK = 128
GLA_CHUNK = 256
RET_CHUNK = 256
MIX_TT = 1024
SCAN_CHUNK = 64
MOE_TR = 256
TOP_K = 2


def _params(semantics):
    return pltpu.CompilerParams(dimension_semantics=semantics, vmem_limit_bytes=VMEM_LIMIT)


def _sigmoid(x):
    return 1.0 / (1.0 + jnp.exp(-x))


def _sigmoid_tanh(x):
    return 0.5 * jnp.tanh(0.5 * x) + 0.5


def _silu(x):
    return x * _sigmoid_tanh(x)


def _softplus(x):
    return jnp.maximum(x, 0.0) + jnp.log1p(jnp.exp(-jnp.abs(x)))


def _dot(a, b):
    return jnp.dot(a, b, preferred_element_type=F32)


def _dot_nt(a, b):
    return lax.dot_general(a, b, (((1,), (1,)), ((), ())), preferred_element_type=F32)


def _dot_tn(a, b):
    return lax.dot_general(a, b, (((0,), (0,)), ((), ())), preferred_element_type=F32)


def _matmul_kslab_kernel(a_ref, b_ref, o_ref, *, slab):
    acc = None
    for c in range(a_ref.shape[1] // slab):
        w = b_ref[pl.ds(c * slab, slab), :].astype(BF16)
        part = _dot(a_ref[:, pl.ds(c * slab, slab)], w)
        acc = part if acc is None else acc + part
    o_ref[...] = acc.astype(o_ref.dtype)


def _matmul_kslab(a, b, lead, *, tm, tn, slab, out_dtype, name):
    m, kdim = a.shape
    n = b.shape[2]
    assert m % tm == 0 and n % tn == 0 and kdim % slab == 0
    return pl.pallas_call(
        functools.partial(_matmul_kslab_kernel, slab=slab),
        grid=(m // tm, n // tn),
        in_specs=[pl.BlockSpec((tm, kdim), lambda i, j: (i, 0)),
                  pl.BlockSpec((None, kdim, tn), lambda i, j: (lead, 0, j))],
        out_specs=pl.BlockSpec((tm, tn), lambda i, j: (i, j)),
        out_shape=jax.ShapeDtypeStruct((m, n), out_dtype),
        compiler_params=_params(("parallel", "parallel")),
        name=name,
    )(a, b)


def _matmul_nt_kernel(a_ref, bt_ref, o_ref, *, sub):
    a = a_ref[...]
    for c in range(bt_ref.shape[1] // sub):
        w = bt_ref[0, pl.ds(c * sub, sub), :].astype(BF16)
        o_ref[:, pl.ds(c * sub, sub)] = _dot_nt(a, w).astype(o_ref.dtype)


def _matmul_nt(a, bt, lead, *, row0, n, tm, tn, out_dtype, name):
    m, kdim = a.shape
    assert m % tm == 0 and n % tn == 0
    return pl.pallas_call(
        functools.partial(_matmul_nt_kernel, sub=min(tn, 256)),
        grid=(m // tm, n // tn),
        in_specs=[pl.BlockSpec((tm, kdim), lambda i, j: (i, 0)),
                  pl.BlockSpec((pl.Element(1), pl.Element(tn), pl.Element(kdim)),
                               lambda i, j: (lead, pl.multiple_of(row0 + j * tn, 16), 0))],
        out_specs=pl.BlockSpec((tm, tn), lambda i, j: (i, j)),
        out_shape=jax.ShapeDtypeStruct((m, n), out_dtype),
        compiler_params=_params(("parallel", "parallel")),
        name=name,
    )(a, bt)


def _swiglu_up_kernel(x_ref, wg_ref, wu_ref, o_ref, wgb_ref, wub_ref):
    @pl.when(pl.program_id(1) == 0)
    def _():
        wgb_ref[...] = wg_ref[...].astype(BF16)
        wub_ref[...] = wu_ref[...].astype(BF16)

    x = x_ref[...]
    g = _dot(x, wgb_ref[...])
    u = _dot(x, wub_ref[...])
    o_ref[...] = (_silu(g) * u).astype(o_ref.dtype)


def _swiglu_up(x, wg, wu, lead, *, tm, tn):
    m, d = x.shape
    f = wg.shape[2]
    wspec = pl.BlockSpec((None, d, tn), lambda j, i: (lead, 0, j))
    return pl.pallas_call(
        _swiglu_up_kernel,
        grid=(f // tn, m // tm),
        in_specs=[pl.BlockSpec((tm, d), lambda j, i: (i, 0)), wspec, wspec],
        out_specs=pl.BlockSpec((tm, tn), lambda j, i: (i, j)),
        out_shape=jax.ShapeDtypeStruct((m, f), BF16),
        scratch_shapes=[pltpu.VMEM((d, tn), BF16), pltpu.VMEM((d, tn), BF16)],
        compiler_params=_params(("parallel", "arbitrary")),
        name="swiglu_up",
    )(x, wg, wu)


def _wres_kernel(*refs, k_sizes):
    n_a = len(k_sizes)
    a_refs = refs[:n_a]
    b_ref, o_ref, wb_ref = refs[n_a:]

    @pl.when(pl.program_id(1) == 0)
    def _():
        wb_ref[...] = b_ref[...].astype(BF16)

    acc = None
    off = 0
    for a_ref, kg in zip(a_refs, k_sizes):
        part = _dot(a_ref[...], wb_ref[pl.ds(off, kg), :])
        acc = part if acc is None else acc + part
        off += kg
    o_ref[...] = acc.astype(o_ref.dtype)


def _matmul_wres(a_list, b, lead, *, n_cols, tm, tn, out_dtype, name):
    m = a_list[0].shape[0]
    k_sizes = tuple(a.shape[1] for a in a_list)
    kdim = sum(k_sizes)
    if lead is None:
        bspec = pl.BlockSpec((kdim, tn), lambda j, i: (0, j))
    else:
        bspec = pl.BlockSpec((None, kdim, tn), lambda j, i: (lead, 0, j))
    return pl.pallas_call(
        functools.partial(_wres_kernel, k_sizes=k_sizes),
        grid=(n_cols // tn, m // tm),
        in_specs=[pl.BlockSpec((tm, kg), lambda j, i: (i, 0)) for kg in k_sizes] + [bspec],
        out_specs=pl.BlockSpec((tm, tn), lambda j, i: (i, j)),
        out_shape=jax.ShapeDtypeStruct((m, n_cols), out_dtype),
        scratch_shapes=[pltpu.VMEM((kdim, tn), BF16)],
        compiler_params=_params(("parallel", "arbitrary")),
        name=name,
    )(*a_list, b)


def _router_kernel(h_ref, w_ref, ei_ref, wt_ref, rk_ref, cnt_ref, carry_ref):
    @pl.when(pl.program_id(0) == 0)
    def _():
        carry_ref[...] = jnp.zeros_like(carry_ref)

    h = h_ref[...]
    w = w_ref[...]
    h_hi = h.astype(BF16)
    h_mid = (h - h_hi.astype(F32)).astype(BF16)
    w_hi = w.astype(BF16)
    w_mid = (w - w_hi.astype(F32)).astype(BF16)
    logits = _dot(h_hi, w_hi) + _dot(h_hi, w_mid) + _dot(h_mid, w_hi)
    tm = logits.shape[0]
    lane = lax.broadcasted_iota(jnp.int32, logits.shape, 1)
    lanef = lane.astype(F32)
    neg = jnp.float32(-jnp.inf)
    logits = jnp.where(lane < N_EXPERTS, logits, neg)
    m1 = jnp.max(logits, axis=1, keepdims=True)
    i1 = jnp.min(jnp.where(logits == m1, lanef, float(LANES)), axis=1, keepdims=True)
    rest = jnp.where(lanef == i1, neg, logits)
    m2 = jnp.max(rest, axis=1, keepdims=True)
    i2 = jnp.min(jnp.where(rest == m2, lanef, float(LANES)), axis=1, keepdims=True)
    e2 = jnp.exp(m2 - m1)
    w1 = 1.0 / (1.0 + e2)
    w2 = e2 / (1.0 + e2)

    sel1 = lanef == i1
    sel2 = lanef == i2
    onehot = jnp.where(sel1 | sel2, 1.0, 0.0)
    t = lax.broadcasted_iota(jnp.int32, (tm, tm), 0)
    s = lax.broadcasted_iota(jnp.int32, (tm, tm), 1)
    before = jnp.where(t > s, 1.0, 0.0).astype(BF16)
    seen = _dot(before, onehot.astype(BF16)) + carry_ref[...]
    r1 = jnp.sum(jnp.where(sel1, seen, 0.0), axis=1, keepdims=True)
    r2 = jnp.sum(jnp.where(sel2, seen, 0.0), axis=1, keepdims=True)
    total = carry_ref[...] + jnp.sum(onehot, axis=0, keepdims=True)
    carry_ref[...] = total

    ei_ref[...] = jnp.where(lane == 0, i1, jnp.where(lane == 1, i2, 0.0)).astype(jnp.int32)
    wt_ref[...] = jnp.where(lane == 0, w1, jnp.where(lane == 1, w2, 0.0))
    rk_ref[...] = jnp.where(lane == 0, r1, jnp.where(lane == 1, r2, 0.0)).astype(jnp.int32)
    cnt_ref[...] = jnp.broadcast_to(total, cnt_ref.shape).astype(jnp.int32)


def _router(h, router_w_padded, *, tm):
    m, d = h.shape
    tok = pl.BlockSpec((tm, LANES), lambda i: (i, 0))
    return pl.pallas_call(
        _router_kernel,
        grid=(m // tm,),
        in_specs=[pl.BlockSpec((tm, d), lambda i: (i, 0)),
                  pl.BlockSpec((d, LANES), lambda i: (0, 0))],
        out_specs=[tok, tok, tok, pl.BlockSpec((SUBLANES, LANES), lambda i: (0, 0))],
        out_shape=[jax.ShapeDtypeStruct((m, LANES), jnp.int32), jax.ShapeDtypeStruct((m, LANES), F32),
                   jax.ShapeDtypeStruct((m, LANES), jnp.int32),
                   jax.ShapeDtypeStruct((SUBLANES, LANES), jnp.int32)],
        scratch_shapes=[pltpu.VMEM((1, LANES), F32)],
        compiler_params=_params(("arbitrary",)),
        name="router",
    )(h, router_w_padded)


def _dispatch_kernel(d1_ref, d2_ref, x_ref, buf_in_ref, o_ref, sem, *, td):
    del buf_in_ref
    base = pl.program_id(0) * td

    def row_copy(src_row, dst_row):
        return pltpu.make_async_copy(x_ref.at[pl.ds(src_row, 1), :], o_ref.at[pl.ds(dst_row, 1), :], sem)

    def issue(t, carry):
        row_copy(t, d1_ref[base + t]).start()
        row_copy(t, d2_ref[base + t]).start()
        return carry

    lax.fori_loop(0, td, issue, 0, unroll=8)

    def drain(t, carry):
        row_copy(0, 0).wait()
        row_copy(0, 0).wait()
        return carry

    lax.fori_loop(0, td, drain, 0, unroll=8)


def _dispatch(xp, dst1, dst2, buf, *, td):
    m = xp.shape[0]
    return pl.pallas_call(
        functools.partial(_dispatch_kernel, td=td),
        grid_spec=pltpu.PrefetchScalarGridSpec(
            num_scalar_prefetch=2, grid=(m // td,),
            in_specs=[pl.BlockSpec((td, xp.shape[1]), lambda i, d1, d2: (i, 0)),
                      pl.BlockSpec(memory_space=pl.ANY)],
            out_specs=pl.BlockSpec(memory_space=pl.ANY),
            scratch_shapes=[pltpu.SemaphoreType.DMA(())]),
        out_shape=jax.ShapeDtypeStruct(buf.shape, buf.dtype),
        input_output_aliases={3: 0},
        compiler_params=_params(("arbitrary",)),
        name="moe_dispatch",
    )(dst1, dst2, xp, buf)


def _unpack_halves(xp):
    lo = lax.bitcast_convert_type(xp << jnp.uint32(16), F32).astype(BF16)
    hi = lax.bitcast_convert_type(xp & jnp.uint32(0xFFFF0000), F32).astype(BF16)
    return lo, hi


def _load_group_weights(j, i, sched, w_hbm_refs, wbuf, wb_refs, sems, *, lead, tn, nj):
    te_ref, tf_ref, _, tnx_ref, tlast_ref, tg_ref, ng_ref = sched

    def fetch(e, jj, slot):
        col = pl.multiple_of(jj * tn, tn)
        for m, w in enumerate(w_hbm_refs):
            pltpu.make_async_copy(w.at[lead, e, :, pl.ds(col, tn)], wbuf.at[slot, m], sems.at[slot]).start()

    @pl.when(tf_ref[i] == 1)
    def _():
        g = j * ng_ref[0] + tg_ref[i]
        slot = g % 2

        @pl.when(g == 0)
        def _():
            fetch(te_ref[i], j, slot)

        last = tlast_ref[i] == 1

        @pl.when(jnp.logical_not(last))
        def _():
            fetch(tnx_ref[i], j, 1 - slot)

        @pl.when(jnp.logical_and(last, j + 1 < nj))
        def _():
            fetch(tnx_ref[i], j + 1, 1 - slot)

        for m, w in enumerate(w_hbm_refs):
            pltpu.make_async_copy(w.at[lead, 0, :, pl.ds(0, tn)], wbuf.at[slot, m], sems.at[slot]).wait()
        for m, wb in enumerate(wb_refs):
            wb[...] = wbuf[slot, m].astype(BF16)


def _moe_up_kernel(*refs, lead, tn, nj):
    sched = refs[:7]
    x_ref, wg_hbm, wu_hbm, o_ref, wbuf, wgb_ref, wub_ref, sems = refs[7:]
    tv_ref = sched[2]
    j = pl.program_id(0)
    i = pl.program_id(1)
    _load_group_weights(j, i, sched, (wg_hbm, wu_hbm), wbuf, (wgb_ref, wub_ref), sems, lead=lead, tn=tn, nj=nj)

    @pl.when(tv_ref[i] == 1)
    def _():
        half = x_ref.shape[1]
        lo, hi = _unpack_halves(x_ref[...])
        g = _dot(lo, wgb_ref[pl.ds(0, half), :]) + _dot(hi, wgb_ref[pl.ds(half, half), :])
        u = _dot(lo, wub_ref[pl.ds(0, half), :]) + _dot(hi, wub_ref[pl.ds(half, half), :])
        o_ref[...] = (_silu(g) * u).astype(o_ref.dtype)

    @pl.when(tv_ref[i] == 0)
    def _():
        o_ref[...] = jnp.zeros_like(o_ref)


def _moe_up(xs, wg, wu, lead, sched, *, tn):
    rows, half = xs.shape
    d = 2 * half
    f = wg.shape[3]
    nj = f // tn
    hbm = pl.BlockSpec(memory_space=pl.ANY)
    return pl.pallas_call(
        functools.partial(_moe_up_kernel, lead=lead, tn=tn, nj=nj),
        grid_spec=pltpu.PrefetchScalarGridSpec(
            num_scalar_prefetch=len(sched), grid=(nj, rows // MOE_TR),
            in_specs=[pl.BlockSpec((MOE_TR, half), lambda j, i, *_: (i, 0)), hbm, hbm],
            out_specs=pl.BlockSpec((MOE_TR, tn), lambda j, i, *_: (i, j)),
            scratch_shapes=[pltpu.VMEM((2, 2, d, tn), F32), pltpu.VMEM((d, tn), BF16), pltpu.VMEM((d, tn), BF16),
                            pltpu.SemaphoreType.DMA((2,))]),
        out_shape=jax.ShapeDtypeStruct((rows, f), BF16),
        compiler_params=_params(("arbitrary", "arbitrary")),
        name="moe_up",
    )(*sched, xs, wg, wu)


def _moe_down_kernel(*refs, lead, tn, nj):
    sched = refs[:7]
    x_ref, w_hbm, o_ref, wbuf, wb_ref, sems = refs[7:]
    tv_ref = sched[2]
    j = pl.program_id(0)
    i = pl.program_id(1)
    _load_group_weights(j, i, sched, (w_hbm,), wbuf, (wb_ref,), sems, lead=lead, tn=tn, nj=nj)

    @pl.when(tv_ref[i] == 1)
    def _():
        o_ref[...] = _dot(x_ref[...], wb_ref[...])

    @pl.when(tv_ref[i] == 0)
    def _():
        o_ref[...] = jnp.zeros_like(o_ref)


def _moe_down(hid, wd, lead, sched, *, tn):
    rows, f = hid.shape
    d = wd.shape[3]
    nj = d // tn
    return pl.pallas_call(
        functools.partial(_moe_down_kernel, lead=lead, tn=tn, nj=nj),
        grid_spec=pltpu.PrefetchScalarGridSpec(
            num_scalar_prefetch=len(sched), grid=(nj, rows // MOE_TR),
            in_specs=[pl.BlockSpec((MOE_TR, f), lambda j, i, *_: (i, 0)), pl.BlockSpec(memory_space=pl.ANY)],
            out_specs=pl.BlockSpec((MOE_TR, tn), lambda j, i, *_: (i, j)),
            scratch_shapes=[pltpu.VMEM((2, 1, f, tn), F32), pltpu.VMEM((f, tn), BF16),
                            pltpu.SemaphoreType.DMA((2,))]),
        out_shape=jax.ShapeDtypeStruct((rows, d), F32),
        compiler_params=_params(("arbitrary", "arbitrary")),
        name="moe_down",
    )(*sched, hid, wd)


def _combine_ln_kernel(d1_ref, d2_ref, y_ref, h_ref, wt_ref, g_ref, b_ref, o_ref, ob_ref,
                       buf1, buf2, sems, *, alpha, tc):
    i = pl.program_id(0)
    slot = i % 2

    def row_copy(src_row, buf, s, t):
        return pltpu.make_async_copy(y_ref.at[pl.ds(src_row, 1), :], buf.at[s, pl.ds(t, 1), :], sems.at[s])

    def gather_tile(step, s):
        base = step * tc

        def issue(t, carry):
            row_copy(d1_ref[base + t], buf1, s, t).start()
            row_copy(d2_ref[base + t], buf2, s, t).start()
            return carry

        lax.fori_loop(0, tc, issue, 0, unroll=8)

    @pl.when(i == 0)
    def _():
        gather_tile(0, 0)

    @pl.when(i + 1 < pl.num_programs(0))
    def _():
        gather_tile(i + 1, 1 - slot)

    def drain(t, carry):
        row_copy(0, buf1, slot, 0).wait()
        row_copy(0, buf2, slot, 0).wait()
        return carry

    lax.fori_loop(0, tc, drain, 0, unroll=8)

    wt = wt_ref[...]
    x = alpha * h_ref[...] + wt[:, 0:1] * buf1[slot] + wt[:, 1:2] * buf2[slot]
    mu = jnp.mean(x, axis=-1, keepdims=True)
    xc = x - mu
    var = jnp.mean(xc * xc, axis=-1, keepdims=True)
    out = xc * lax.rsqrt(var + LN_EPS) * g_ref[...] + b_ref[...]
    o_ref[...] = out
    ob_ref[...] = out.astype(BF16)


def _combine_ln(y, dst1, dst2, h, wt, g, b, *, alpha, tc):
    m, d = h.shape
    row = pl.BlockSpec((tc, d), lambda i, d1, d2: (i, 0))
    vec = pl.BlockSpec((1, d), lambda i, d1, d2: (0, 0))
    return pl.pallas_call(
        functools.partial(_combine_ln_kernel, alpha=alpha, tc=tc),
        grid_spec=pltpu.PrefetchScalarGridSpec(
            num_scalar_prefetch=2, grid=(m // tc,),
            in_specs=[pl.BlockSpec(memory_space=pl.ANY), row,
                      pl.BlockSpec((tc, LANES), lambda i, d1, d2: (i, 0)), vec, vec],
            out_specs=[row, row],
            scratch_shapes=[pltpu.VMEM((2, tc, d), F32), pltpu.VMEM((2, tc, d), F32),
                            pltpu.SemaphoreType.DMA((2,))]),
        out_shape=[jax.ShapeDtypeStruct((m, d), F32), jax.ShapeDtypeStruct((m, d), BF16)],
        compiler_params=_params(("arbitrary",)),
        name="moe_combine_ln",
    )(dst1, dst2, y, h, wt, g.reshape(1, d), b.reshape(1, d))


def _add_ln_kernel(h_ref, y_ref, g_ref, b_ref, o_ref, ob_ref, *maybe_packed_ref, alpha):
    x = alpha * h_ref[...] + y_ref[...]
    mu = jnp.mean(x, axis=-1, keepdims=True)
    xc = x - mu
    var = jnp.mean(xc * xc, axis=-1, keepdims=True)
    out = xc * lax.rsqrt(var + LN_EPS) * g_ref[...] + b_ref[...]
    o_ref[...] = out
    out_b = out.astype(BF16)
    ob_ref[...] = out_b
    if maybe_packed_ref:
        half = out.shape[1] // 2
        as_u32 = lax.bitcast_convert_type(out_b.astype(F32), jnp.uint32)
        maybe_packed_ref[0][...] = (as_u32[:, :half] >> jnp.uint32(16)) | as_u32[:, half:]


def _add_ln(h, y, g, b, *, alpha, tm, packed=False):
    m, d = h.shape
    row = pl.BlockSpec((tm, d), lambda i: (i, 0))
    vec = pl.BlockSpec((1, d), lambda i: (0, 0))
    out_specs = [row, row]
    out_shape = [jax.ShapeDtypeStruct((m, d), F32), jax.ShapeDtypeStruct((m, d), BF16)]
    if packed:
        out_specs.append(pl.BlockSpec((tm, d // 2), lambda i: (i, 0)))
        out_shape.append(jax.ShapeDtypeStruct((m, d // 2), jnp.uint32))
    return pl.pallas_call(
        functools.partial(_add_ln_kernel, alpha=alpha),
        grid=(m // tm,),
        in_specs=[row, row, vec, vec],
        out_specs=out_specs,
        out_shape=out_shape,
        compiler_params=_params(("parallel",)),
        name="add_ln",
    )(h, y, g.reshape(1, d), b.reshape(1, d))


def _level_masks(c):
    t = lax.broadcasted_iota(jnp.int32, (c, c), 0)
    s = lax.broadcasted_iota(jnp.int32, (c, c), 1)
    x = t ^ s
    masks = [t == s]
    for p in range(1, int(math.log2(c)) + 1):
        masks.append((t > s) & ((x >> (p - 1)) == 1))
    return masks


def _midpoint_rows(cum_ref, c, k, p):
    blk = 1 << p
    half = blk >> 1
    if blk >= SUBLANES:
        parts = [jnp.broadcast_to(cum_ref[pl.ds(b * blk + half - 1, 1), :], (blk, k))
                 for b in range(c // blk)]
        return parts[0] if len(parts) == 1 else jnp.concatenate(parts, axis=0)
    sub = lax.broadcasted_iota(jnp.int32, (c, k), 0) & (SUBLANES - 1)
    res = None
    for j in range(SUBLANES // blk):
        rows = jnp.concatenate(
            [jnp.broadcast_to(cum_ref[pl.ds(SUBLANES * v + j * blk + half - 1, 1), :], (SUBLANES, k))
             for v in range(c // SUBLANES)], axis=0)
        res = rows if res is None else jnp.where(sub >= j * blk, rows, res)
    return res


def _split3(x):
    hi = x.astype(BF16)
    r1 = x - hi.astype(F32)
    mid = r1.astype(BF16)
    lo = (r1 - mid.astype(F32)).astype(BF16)
    return hi, mid, lo


def _gla_chunk(q, k, v, g, state_ref, cum_ref, masks, tri):
    c, kd = q.shape
    g_hi, g_mid, g_lo = _split3(g)
    cum = _dot(tri, g_hi) + _dot(tri, g_mid) + _dot(tri, g_lo)
    cum_ref[...] = cum
    row = lax.broadcasted_iota(jnp.int32, (c, kd), 0)
    qb = q.astype(BF16)
    kb = k.astype(BF16)
    vb = v.astype(BF16)
    scores = jnp.where(masks[0], _dot_nt(qb, kb), 0.0)
    for p in range(1, len(masks)):
        mid = _midpoint_rows(cum_ref, c, kd, p)
        upper = ((row >> (p - 1)) & 1) == 1
        z = (jnp.where(upper, q, k) * jnp.exp(-jnp.abs(cum - mid))).astype(BF16)
        scores = jnp.where(masks[p], _dot_nt(z, z), scores)
    state = state_ref[...]
    out = _dot(scores.astype(BF16), vb)
    out = out + _dot_nt((q * jnp.exp(cum)).astype(BF16), state.astype(BF16))
    last = cum_ref[pl.ds(c - 1, 1), :]
    k_dec = (k * jnp.exp(last - cum)).astype(BF16)
    state_ref[...] = state * jnp.exp(last) + _dot_tn(vb, k_dec)
    return out


def _tri_ones(c):
    t = lax.broadcasted_iota(jnp.int32, (c, c), 0)
    s = lax.broadcasted_iota(jnp.int32, (c, c), 1)
    return jnp.where(t >= s, 1.0, 0.0).astype(BF16)


def _hgrn_kernel(q_ref, f_ref, i_ref, g_ref, lb_ref, nw_ref, o_ref, state_ref, cum_ref, *, layer, chunk):
    @pl.when(pl.program_id(2) == 0)
    def _():
        state_ref[...] = jnp.zeros_like(state_ref)

    lb_all = lb_ref[...]
    e = jnp.exp(lb_all - jnp.max(lb_all, axis=0, keepdims=True))
    soft = e / jnp.sum(e, axis=0, keepdims=True)
    lower = jnp.zeros((1, LANES), F32)
    for l in range(1, layer + 1):
        lower = lower + soft[l:l + 1, :]
    masks = _level_masks(chunk)
    tri = _tri_ones(chunk)
    kd = q_ref.shape[1]
    for c in range(q_ref.shape[0] // chunk):
        rows = pl.ds(c * chunk, chunk)
        f = lower + (1.0 - lower) * _sigmoid(f_ref[rows, :])
        q = _silu(q_ref[rows, :]) * (kd ** -0.5)
        o = _gla_chunk(q, 1.0 - f, i_ref[rows, :], jnp.log(f), state_ref, cum_ref, masks, tri)
        o = o * lax.rsqrt(jnp.mean(o * o, axis=-1, keepdims=True) + RMS_EPS) * nw_ref[...]
        o_ref[rows, :] = (o * _silu(g_ref[rows, :])).astype(o_ref.dtype)


def _hgrn(proj, hgrn_lb, norm_w, *, layer, batch, seq):
    nt = seq // MIX_TT

    def col(cb):
        return pl.BlockSpec((MIX_TT, LANES), lambda b, h, t: (b * nt + t, cb + h))

    return pl.pallas_call(
        functools.partial(_hgrn_kernel, layer=layer, chunk=HGRN_CHUNK),
        grid=(batch, HGRN_HEADS, nt),
        in_specs=[col(CB_A_Q), col(CB_A_F), col(CB_A_I), col(CB_A_G),
                  pl.BlockSpec((hgrn_lb.shape[0], LANES), lambda b, h, t: (0, h)),
                  pl.BlockSpec((1, LANES), lambda b, h, t: (0, 0))],
        out_specs=pl.BlockSpec((MIX_TT, LANES), lambda b, h, t: (b * nt + t, h)),
        out_shape=jax.ShapeDtypeStruct((batch * seq, GROUP_WIDTH), BF16),
        scratch_shapes=[pltpu.VMEM((LANES, LANES), F32), pltpu.VMEM((HGRN_CHUNK, LANES), F32)],
        compiler_params=_params(("parallel", "parallel", "arbitrary")),
        name="hgrn",
    )(proj, proj, proj, proj, hgrn_lb, norm_w.reshape(1, LANES))


def _gla_kernel(q_ref, k_ref, v_ref, g_ref, r_ref, wg_ref, bg_ref, nw_ref, o_ref, state_ref, cum_ref, *, chunk):
    @pl.when(pl.program_id(2) == 0)
    def _():
        state_ref[...] = jnp.zeros_like(state_ref)

    masks = _level_masks(chunk)
    tri = _tri_ones(chunk)
    kd = q_ref.shape[1]
    for c in range(q_ref.shape[0] // chunk):
        rows = pl.ds(c * chunk, chunk)
        pre = _dot(r_ref[rows, :], wg_ref[...]) + bg_ref[...]
        log_alpha = (jnp.minimum(pre, 0.0) - jnp.log1p(jnp.exp(-jnp.abs(pre)))) * (1.0 / GLA_NORMALIZER)
        o = _gla_chunk(q_ref[rows, :] * (kd ** -0.5), k_ref[rows, :], v_ref[rows, :], log_alpha,
                       state_ref, cum_ref, masks, tri)
        o = o * lax.rsqrt(jnp.mean(o * o, axis=-1, keepdims=True) + RMS_EPS) * nw_ref[...]
        o_ref[rows, :] = (o * _silu(g_ref[rows, :])).astype(o_ref.dtype)


def _gla(proj, proj_r, w_gate_padded, b_gate, norm_w, *, batch, seq):
    nt = seq // MIX_TT
    vb = GLA_VAL // LANES

    return pl.pallas_call(
        functools.partial(_gla_kernel, chunk=GLA_CHUNK),
        grid=(batch, GLA_HEADS, nt),
        in_specs=[pl.BlockSpec((MIX_TT, GLA_KEY), lambda b, h, t: (b * nt + t, CB_C_Q + h)),
                  pl.BlockSpec((MIX_TT, GLA_KEY), lambda b, h, t: (b * nt + t, CB_C_K + h)),
                  pl.BlockSpec((MIX_TT, GLA_VAL), lambda b, h, t: (b * nt + t, CB_C_V // vb + h)),
                  pl.BlockSpec((MIX_TT, GLA_VAL), lambda b, h, t: (b * nt + t, CB_C_G // vb + h)),
                  pl.BlockSpec((MIX_TT, LANES), lambda b, h, t: (b * nt + t, 0)),
                  pl.BlockSpec((LANES, GLA_KEY), lambda b, h, t: (0, h)),
                  pl.BlockSpec((1, GLA_KEY), lambda b, h, t: (0, h)),
                  pl.BlockSpec((1, GLA_VAL), lambda b, h, t: (0, 0))],
        out_specs=pl.BlockSpec((MIX_TT, GLA_VAL), lambda b, h, t: (b * nt + t, h)),
        out_shape=jax.ShapeDtypeStruct((batch * seq, GROUP_WIDTH), BF16),
        scratch_shapes=[pltpu.VMEM((GLA_VAL, GLA_KEY), F32), pltpu.VMEM((GLA_CHUNK, GLA_KEY), F32)],
        compiler_params=_params(("parallel", "parallel", "arbitrary")),
        name="gla",
    )(proj, proj, proj, proj, proj_r, w_gate_padded, b_gate.reshape(1, -1), norm_w.reshape(1, GLA_VAL))


def _ret_kernel(q_ref, k_ref, v_ref, g_ref, cos_ref, sin_ref, dmat_ref, qdec_ref, kdec_ref, cdec_ref, nw_ref,
                o_ref, state_ref, *, chunk):
    @pl.when(pl.program_id(2) == 0)
    def _():
        state_ref[...] = jnp.zeros_like(state_ref)

    kd = q_ref.shape[1]
    half = kd // 2
    dmat = dmat_ref[...]
    qdec = qdec_ref[...]
    kdec = kdec_ref[...]
    cdec = cdec_ref[...]
    for c in range(q_ref.shape[0] // chunk):
        rows = pl.ds(c * chunk, chunk)
        cos = cos_ref[rows, :]
        sin = sin_ref[rows, :]
        q = q_ref[rows, :]
        k = k_ref[rows, :]
        q = q * cos + pltpu.roll(q, half, 1) * sin
        k = (k * cos + pltpu.roll(k, half, 1) * sin) * (kd ** -0.5)
        vb = v_ref[rows, :].astype(BF16)
        state = state_ref[...]
        scores = _dot_nt(q.astype(BF16), k.astype(BF16)) * dmat
        o = _dot(scores.astype(BF16), vb) + _dot_nt((q * qdec).astype(BF16), state.astype(BF16))
        state_ref[...] = state * cdec + _dot_tn(vb, (k * kdec).astype(BF16))
        mu = jnp.mean(o, axis=-1, keepdims=True)
        oc = o - mu
        var = jnp.mean(oc * oc, axis=-1, keepdims=True)
        o = oc * lax.rsqrt(var + LN_EPS) * nw_ref[...]
        o_ref[rows, :] = (o * _silu(g_ref[rows, :])).astype(o_ref.dtype)


def _retention_tables(seq, chunk):
    half = LANES // 2
    inv_freq = ROPE_BASE ** (-jnp.arange(half, dtype=F32) / half)
    ang = jnp.arange(seq, dtype=F32)[:, None] * inv_freq[None, :]
    cos, sin = jnp.cos(ang), jnp.sin(ang)
    cos2 = jnp.concatenate([cos, cos], axis=1)
    sin2 = jnp.concatenate([-sin, sin], axis=1)
    log_gamma = jnp.log1p(-(2.0 ** (-5.0 - jnp.arange(RET_HEADS, dtype=F32))))
    pos = jnp.arange(chunk, dtype=F32)
    rel = pos[:, None] - pos[None, :]
    dmat = jnp.where(rel >= 0, jnp.exp(log_gamma[:, None, None] * jnp.maximum(rel, 0.0)), 0.0)
    qdec = jnp.exp(log_gamma[:, None] * (pos + 1.0)[None, :])
    kdec = jnp.exp(log_gamma[:, None] * (chunk - 1.0 - pos)[None, :])
    cdec = jnp.exp(log_gamma * chunk)
    bcast = lambda a: jnp.broadcast_to(a[..., None], a.shape + (LANES,))
    return cos2, sin2, dmat, bcast(qdec), bcast(kdec), bcast(cdec[:, None])


def _retention(proj, tables, norm_w, *, batch, seq):
    nt = seq // MIX_TT
    cos2, sin2, dmat, qdec, kdec, cdec = tables
    chunk = dmat.shape[1]

    def col(cb):
        return pl.BlockSpec((MIX_TT, LANES), lambda b, h, t: (b * nt + t, cb + h))

    rot = pl.BlockSpec((MIX_TT, LANES), lambda b, h, t: (t, 0))
    return pl.pallas_call(
        functools.partial(_ret_kernel, chunk=chunk),
        grid=(batch, RET_HEADS, nt),
        in_specs=[col(CB_D_Q), col(CB_D_K), col(CB_D_V), col(CB_D_G), rot, rot,
                  pl.BlockSpec((None, chunk, chunk), lambda b, h, t: (h, 0, 0)),
                  pl.BlockSpec((None, chunk, LANES), lambda b, h, t: (h, 0, 0)),
                  pl.BlockSpec((None, chunk, LANES), lambda b, h, t: (h, 0, 0)),
                  pl.BlockSpec((None, 1, LANES), lambda b, h, t: (h, 0, 0)),
                  pl.BlockSpec((1, LANES), lambda b, h, t: (0, 0))],
        out_specs=pl.BlockSpec((MIX_TT, LANES), lambda b, h, t: (b * nt + t, h)),
        out_shape=jax.ShapeDtypeStruct((batch * seq, GROUP_WIDTH), BF16),
        scratch_shapes=[pltpu.VMEM((LANES, LANES), F32)],
        compiler_params=_params(("parallel", "parallel", "arbitrary")),
        name="retention",
    )(proj, proj, proj, proj, cos2, sin2, dmat, qdec, kdec, cdec, norm_w.reshape(1, LANES))


def _lru_kernel(x_ref, y_ref, cw_ref, cb_ref, wa_ref, ba_ref, wx_ref, bx_ref, lam_ref, o_ref,
                xbuf_ref, a_ref, u_ref, carry_ref, *, scan_chunk):
    tt = x_ref.shape[0]

    @pl.when(pl.program_id(2) == 0)
    def _():
        xbuf_ref[pl.ds(0, SUBLANES), :] = jnp.zeros((SUBLANES, LANES), F32)
        carry_ref[...] = jnp.zeros_like(carry_ref)

    xbuf_ref[pl.ds(SUBLANES, tt), :] = x_ref[...]
    xc = cb_ref[...]
    for j in range(CONV_WIDTH):
        xc = xc + cw_ref[pl.ds(j, 1), :] * xbuf_ref[pl.ds(SUBLANES - (CONV_WIDTH - 1) + j, tt), :]
    xbuf_ref[pl.ds(0, SUBLANES), :] = x_ref[pl.ds(tt - SUBLANES, SUBLANES), :]

    xcb = xc.astype(BF16)
    r = _sigmoid_tanh(_dot(xcb, wa_ref[...].astype(BF16)) + ba_ref[...])
    i = _sigmoid_tanh(_dot(xcb, wx_ref[...].astype(BF16)) + bx_ref[...])
    log_a = (-LRU_C) * r * _softplus(-lam_ref[...])
    a_ref[...] = jnp.exp(log_a)
    th = jnp.tanh(log_a)
    u_ref[...] = jnp.sqrt(-2.0 * th / (1.0 - th)) * (i * xc)

    row = lax.broadcasted_iota(jnp.int32, (scan_chunk, LANES), 0)
    for c in range(tt // scan_chunk):
        rows = pl.ds(c * scan_chunk, scan_chunk)
        a = a_ref[rows, :]
        u = u_ref[rows, :]
        shift = 1
        while shift < scan_chunk:
            valid = row >= shift
            u = jnp.where(valid, a * pltpu.roll(u, shift, 0) + u, u)
            a = jnp.where(valid, a * pltpu.roll(a, shift, 0), a)
            shift *= 2
        h = a * carry_ref[...] + u
        carry_ref[...] = h[scan_chunk - 1:scan_chunk, :]
        o_ref[rows, :] = (h * jax.nn.gelu(y_ref[rows, :])).astype(o_ref.dtype)


def _rg_lru(proj, conv_w, conv_b, wa, ba, wx, bx, lam, *, batch, seq):
    nt = seq // MIX_TT
    vec = pl.BlockSpec((1, LANES), lambda b, n, t: (0, n))
    mat = pl.BlockSpec((None, LANES, LANES), lambda b, n, t: (n, 0, 0))
    return pl.pallas_call(
        functools.partial(_lru_kernel, scan_chunk=SCAN_CHUNK),
        grid=(batch, LRU_BLOCKS, nt),
        in_specs=[pl.BlockSpec((MIX_TT, LANES), lambda b, n, t: (b * nt + t, CB_B_X + n)),
                  pl.BlockSpec((MIX_TT, LANES), lambda b, n, t: (b * nt + t, CB_B_Y + n)),
                  pl.BlockSpec((CONV_WIDTH, LANES), lambda b, n, t: (0, n)),
                  vec, mat, vec, mat, vec, vec],
        out_specs=pl.BlockSpec((MIX_TT, LANES), lambda b, n, t: (b * nt + t, n)),
        out_shape=jax.ShapeDtypeStruct((batch * seq, GROUP_WIDTH), BF16),
        scratch_shapes=[pltpu.VMEM((MIX_TT + SUBLANES, LANES), F32),
                        pltpu.VMEM((MIX_TT, LANES), F32),
                        pltpu.VMEM((MIX_TT, LANES), F32),
                        pltpu.VMEM((1, LANES), F32)],
        compiler_params=_params(("parallel", "parallel", "arbitrary")),
        name="rg_lru",
    )(proj, proj, conv_w, conv_b.reshape(1, -1), wa, ba.reshape(1, -1), wx, bx.reshape(1, -1),
      lam.reshape(1, -1))


def _moe_ffn_ln(h, hp, sorted_buf, router_w, exp_w_gate, exp_w_up, exp_w_down, lead, ln_g, ln_b, *, alpha):
    m, d = h.shape
    rw = jnp.concatenate([router_w, jnp.zeros((d, LANES - N_EXPERTS), F32)], axis=1)
    ei, wt, rk, cnt = _router(h, rw, tm=256)

    counts = cnt[0, :N_EXPERTS]
    padded = ((counts + MOE_TR - 1) // MOE_TR) * MOE_TR
    ends = jnp.cumsum(padded)
    offs = ends - padded
    dst1 = offs[ei[:, 0]] + rk[:, 0]
    dst2 = offs[ei[:, 1]] + rk[:, 1]
    starts = jnp.arange(sorted_buf.shape[0] // MOE_TR, dtype=jnp.int32) * MOE_TR
    te = jnp.minimum(jnp.sum(starts[:, None] >= ends[None, :], axis=1), N_EXPERTS - 1).astype(jnp.int32)
    tv = (starts < ends[N_EXPERTS - 1]).astype(jnp.int32)
    prev = jnp.concatenate([jnp.full((1,), -1, jnp.int32), te[:-1]])
    tf = (te != prev).astype(jnp.int32) * tv
    n_tiles = starts.shape[0]
    tg = (jnp.cumsum(tf) - 1).astype(jnp.int32)
    ng = jnp.sum(tf).astype(jnp.int32).reshape(1)
    group_start = jnp.where(tf == 1, jnp.arange(n_tiles, dtype=jnp.int32), n_tiles)
    next_start = lax.cummin(group_start, axis=0, reverse=True)
    next_start = jnp.concatenate([next_start[1:], jnp.full((1,), n_tiles, jnp.int32)])
    tlast = (next_start >= n_tiles).astype(jnp.int32)
    tnx = jnp.where(tlast == 1, te[0], te[jnp.minimum(next_start, n_tiles - 1)]).astype(jnp.int32)
    sched = (te, tf, tv, tnx, tlast, tg, ng)

    sorted_buf = _dispatch(hp, dst1, dst2, sorted_buf, td=256)
    hid = _moe_up(sorted_buf, exp_w_gate, exp_w_up, lead, sched, tn=512)
    y = _moe_down(hid, exp_w_down, lead, sched, tn=2048)
    h, hb = _combine_ln(y, dst1, dst2, h, wt, ln_g, ln_b, alpha=alpha, tc=256)
    return h, hb, sorted_buf


def kernel(x, w_in, w_out, hgrn_lb, hgrn_norm, conv_w, conv_b, lru_wa, lru_ba, lru_wx, lru_bx, lru_lambda,
           gla_w_gate, gla_b_gate, gla_norm, ret_norm, ln1_g, ln1_b, ln2_g, ln2_b, ffn_w_gate, ffn_w_up,
           ffn_w_down, router_w, exp_w_gate, exp_w_up, exp_w_down):
    batch, seq, d = x.shape
    depth = w_in.shape[0]
    m = batch * seq
    alpha = (2 * depth) ** 0.25
    tables = _retention_tables(seq, RET_CHUNK)

    sorted_rows = TOP_K * m + N_EXPERTS * MOE_TR
    sorted_buf = jnp.zeros((sorted_rows, d // 2), jnp.uint32)

    w_in_t = jnp.swapaxes(w_in, 1, 2)
    w_r = jnp.pad(w_in_t[:, ABC_COLS:D_COL0, :], ((0, 0), (0, LANES - GLA_RANK), (0, 0)))

    h = x.reshape(m, d)
    hb = h.astype(BF16)
    for layer in range(depth):
        proj = _matmul_nt(hb, w_in_t, layer, row0=0, n=ABC_COLS, tm=1024, tn=512, out_dtype=F32, name="in_proj")
        proj_d = _matmul_nt(hb, w_in_t, layer, row0=D_COL0, n=d, tm=1024, tn=512, out_dtype=F32,
                            name="in_proj_ret")
        proj_r = _matmul_nt(hb, w_r, layer, row0=0, n=LANES, tm=1024, tn=LANES, out_dtype=F32,
                            name="in_proj_gate")

        w_gate_p = jnp.concatenate(
            [gla_w_gate[layer], jnp.zeros((LANES - GLA_RANK, gla_w_gate.shape[2]), F32)], axis=0)
        o_a = _hgrn(proj, hgrn_lb, hgrn_norm[layer], layer=layer, batch=batch, seq=seq)
        o_b = _rg_lru(proj, conv_w[layer], conv_b[layer], lru_wa[layer], lru_ba[layer], lru_wx[layer],
                      lru_bx[layer], lru_lambda[layer], batch=batch, seq=seq)
        o_c = _gla(proj, proj_r, w_gate_p, gla_b_gate[layer], gla_norm[layer], batch=batch, seq=seq)
        o_d = _retention(proj_d, tables, ret_norm[layer], batch=batch, seq=seq)
        mix = _matmul_wres([o_a, o_b, o_c, o_d], w_out, layer, n_cols=d, tm=1024, tn=512, out_dtype=BF16,
                           name="out_proj")

        j = layer // 2
        if layer % 2 == 0:
            h, hb = _add_ln(h, mix, ln1_g[layer], ln1_b[layer], alpha=alpha, tm=256)
            hid = _swiglu_up(hb, ffn_w_gate, ffn_w_up, j, tm=1024, tn=256)
            ff = _matmul_kslab(hid, ffn_w_down, j, tm=1024, tn=256, slab=2048, out_dtype=BF16, name="ffn_down")
            h, hb = _add_ln(h, ff, ln2_g[layer], ln2_b[layer], alpha=alpha, tm=256)
        else:
            h, hb, hp = _add_ln(h, mix, ln1_g[layer], ln1_b[layer], alpha=alpha, tm=256, packed=True)
            h, hb, sorted_buf = _moe_ffn_ln(h, hp, sorted_buf, router_w[j], exp_w_gate, exp_w_up, exp_w_down, j,
                                            ln2_g[layer], ln2_b[layer], alpha=alpha)
    return h.reshape(batch, seq, d)
```

```python
import functools
import math

import jax
import jax.numpy as jnp
from jax import lax
from jax.experimental import pallas as pl
from jax.experimental.pallas import tpu as pltpu

F32 = jnp.float32
BF16 = jnp.bfloat16

LANES = 128
SUBLANES = 8
VMEM_LIMIT = 56 * 1024 * 1024

D_MODEL = 4096
N_GROUPS = 4
GROUP_WIDTH = D_MODEL // N_GROUPS
HGRN_HEADS = 8
LRU_BLOCKS = 8
CONV_WIDTH = 4
LRU_C = 8.0
GLA_HEADS = 4
GLA_KEY = 128
GLA_VAL = 256
GLA_RANK = 16
GLA_NORMALIZER = 16.0
RET_HEADS = 8
ROPE_BASE = 10000.0
N_EXPERTS = 8
LN_EPS = 1e-5
RMS_EPS = 1e-6

CB_A_Q, CB_A_F, CB_A_I, CB_A_G = 0, 8, 16, 24
CB_B_X, CB_B_Y = 32, 40
CB_C_Q, CB_C_K, CB_C_V, CB_C_G = 48, 52, 56, 64
CB_D_Q, CB_D_K, CB_D_V, CB_D_G = 0, 8, 16, 24
ABC_COLS = 9216
D_COL0 = ABC_COLS + GLA_RANK

HGRN_CHUNK = 128
GLA_CHUNK = 256
RET_CHUNK = 256
MIX_TT = 2048
SCAN_CHUNK = 64
MOE_TR = 256
TOP_K = 2


def _params(semantics):
    return pltpu.CompilerParams(dimension_semantics=semantics, vmem_limit_bytes=VMEM_LIMIT)


def _sigmoid(x):
    return 1.0 / (1.0 + jnp.exp(-x))


def _sigmoid_tanh(x):
    return 0.5 * jnp.tanh(0.5 * x) + 0.5


def _silu(x):
    return x * _sigmoid_tanh(x)


def _softplus(x):
    return jnp.maximum(x, 0.0) + jnp.log1p(jnp.exp(-jnp.abs(x)))


def _dot(a, b):
    return jnp.dot(a, b, preferred_element_type=F32)


def _dot_nt(a, b):
    return lax.dot_general(a, b, (((1,), (1,)), ((), ())), preferred_element_type=F32)


def _dot_tn(a, b):
    return lax.dot_general(a, b, (((0,), (0,)), ((), ())), preferred_element_type=F32)


def _matmul_kslab_kernel(a_ref, b_ref, o_ref, *, slab):
    acc = None
    for c in range(a_ref.shape[1] // slab):
        w = b_ref[pl.ds(c * slab, slab), :].astype(BF16)
        part = _dot(a_ref[:, pl.ds(c * slab, slab)], w)
        acc = part if acc is None else acc + part
    o_ref[...] = acc.astype(o_ref.dtype)


def _matmul_kslab(a, b, lead, *, tm, tn, slab, out_dtype, name):
    m, kdim = a.shape
    n = b.shape[2]
    assert m % tm == 0 and n % tn == 0 and kdim % slab == 0
    return pl.pallas_call(
        functools.partial(_matmul_kslab_kernel, slab=slab),
        grid=(m // tm, n // tn),
        in_specs=[pl.BlockSpec((tm, kdim), lambda i, j: (i, 0)),
                  pl.BlockSpec((None, kdim, tn), lambda i, j: (lead, 0, j))],
        out_specs=pl.BlockSpec((tm, tn), lambda i, j: (i, j)),
        out_shape=jax.ShapeDtypeStruct((m, n), out_dtype),
        compiler_params=_params(("parallel", "parallel")),
        name=name,
    )(a, b)


def _matmul_nt_kernel(a_ref, bt_ref, o_ref, *, sub):
    a = a_ref[...]
    for c in range(bt_ref.shape[1] // sub):
        w = bt_ref[0, pl.ds(c * sub, sub), :].astype(BF16)
        o_ref[:, pl.ds(c * sub, sub)] = _dot_nt(a, w).astype(o_ref.dtype)


def _matmul_nt(a, bt, lead, *, row0, n, tm, tn, out_dtype, name):
    m, kdim = a.shape
    assert m % tm == 0 and n % tn == 0
    return pl.pallas_call(
        functools.partial(_matmul_nt_kernel, sub=min(tn, 256)),
        grid=(m // tm, n // tn),
        in_specs=[pl.BlockSpec((tm, kdim), lambda i, j: (i, 0)),
                  pl.BlockSpec((pl.Element(1), pl.Element(tn), pl.Element(kdim)),
                               lambda i, j: (lead, pl.multiple_of(row0 + j * tn, 16), 0))],
        out_specs=pl.BlockSpec((tm, tn), lambda i, j: (i, j)),
        out_shape=jax.ShapeDtypeStruct((m, n), out_dtype),
        compiler_params=_params(("parallel", "parallel")),
        name=name,
    )(a, bt)


def _swiglu_up_kernel(x_ref, wg_ref, wu_ref, o_ref, wgb_ref, wub_ref):
    @pl.when(pl.program_id(1) == 0)
    def _():
        wgb_ref[...] = wg_ref[...].astype(BF16)
        wub_ref[...] = wu_ref[...].astype(BF16)

    x = x_ref[...]
    g = _dot(x, wgb_ref[...])
    u = _dot(x, wub_ref[...])
    o_ref[...] = (_silu(g) * u).astype(o_ref.dtype)


def _swiglu_up(x, wg, wu, lead, *, tm, tn):
    m, d = x.shape
    f = wg.shape[2]
    wspec = pl.BlockSpec((None, d, tn), lambda j, i: (lead, 0, j))
    return pl.pallas_call(
        _swiglu_up_kernel,
        grid=(f // tn, m // tm),
        in_specs=[pl.BlockSpec((tm, d), lambda j, i: (i, 0)), wspec, wspec],
        out_specs=pl.BlockSpec((tm, tn), lambda j, i: (i, j)),
        out_shape=jax.ShapeDtypeStruct((m, f), BF16),
        scratch_shapes=[pltpu.VMEM((d, tn), BF16), pltpu.VMEM((d, tn), BF16)],
        compiler_params=_params(("parallel", "arbitrary")),
        name="swiglu_up",
    )(x, wg, wu)


def _wres_kernel(*refs, k_sizes):
    n_a = len(k_sizes)
    a_refs = refs[:n_a]
    b_ref, o_ref, wb_ref = refs[n_a:]

    @pl.when(pl.program_id(1) == 0)
    def _():
        wb_ref[...] = b_ref[...].astype(BF16)

    acc = None
    off = 0
    for a_ref, kg in zip(a_refs, k_sizes):
        part = _dot(a_ref[...], wb_ref[pl.ds(off, kg), :])
        acc = part if acc is None else acc + part
        off += kg
    o_ref[...] = acc.astype(o_ref.dtype)


def _matmul_wres(a_list, b, lead, *, n_cols, tm, tn, out_dtype, name):
    m = a_list[0].shape[0]
    k_sizes = tuple(a.shape[1] for a in a_list)
    kdim = sum(k_sizes)
    if lead is None:
        bspec = pl.BlockSpec((kdim, tn), lambda j, i: (0, j))
    else:
        bspec = pl.BlockSpec((None, kdim, tn), lambda j, i: (lead, 0, j))
    return pl.pallas_call(
        functools.partial(_wres_kernel, k_sizes=k_sizes),
        grid=(n_cols // tn, m // tm),
        in_specs=[pl.BlockSpec((tm, kg), lambda j, i: (i, 0)) for kg in k_sizes] + [bspec],
        out_specs=pl.BlockSpec((tm, tn), lambda j, i: (i, j)),
        out_shape=jax.ShapeDtypeStruct((m, n_cols), out_dtype),
        scratch_shapes=[pltpu.VMEM((kdim, tn), BF16)],
        compiler_params=_params(("parallel", "arbitrary")),
        name=name,
    )(*a_list, b)


def _router_kernel(h_ref, w_ref, ei_ref, wt_ref, rk_ref, cnt_ref, carry_ref):
    @pl.when(pl.program_id(0) == 0)
    def _():
        carry_ref[...] = jnp.zeros_like(carry_ref)

    h = h_ref[...]
    w = w_ref[...]
    h_hi = h.astype(BF16)
    h_mid = (h - h_hi.astype(F32)).astype(BF16)
    w_hi = w.astype(BF16)
    w_mid = (w - w_hi.astype(F32)).astype(BF16)
    logits = _dot(h_hi, w_hi) + _dot(h_hi, w_mid) + _dot(h_mid, w_hi)
    tm = logits.shape[0]
    lane = lax.broadcasted_iota(jnp.int32, logits.shape, 1)
    lanef = lane.astype(F32)
    neg = jnp.float32(-jnp.inf)
    logits = jnp.where(lane < N_EXPERTS, logits, neg)
    m1 = jnp.max(logits, axis=1, keepdims=True)
    i1 = jnp.min(jnp.where(logits == m1, lanef, float(LANES)), axis=1, keepdims=True)
    rest = jnp.where(lanef == i1, neg, logits)
    m2 = jnp.max(rest, axis=1, keepdims=True)
    i2 = jnp.min(jnp.where(rest == m2, lanef, float(LANES)), axis=1, keepdims=True)
    e2 = jnp.exp(m2 - m1)
    w1 = 1.0 / (1.0 + e2)
    w2 = e2 / (1.0 + e2)

    sel1 = lanef == i1
    sel2 = lanef == i2
    onehot = jnp.where(sel1 | sel2, 1.0, 0.0)
    t = lax.broadcasted_iota(jnp.int32, (tm, tm), 0)
    s = lax.broadcasted_iota(jnp.int32, (tm, tm), 1)
    before = jnp.where(t > s, 1.0, 0.0).astype(BF16)
    seen = _dot(before, onehot.astype(BF16)) + carry_ref[...]
    r1 = jnp.sum(jnp.where(sel1, seen, 0.0), axis=1, keepdims=True)
    r2 = jnp.sum(jnp.where(sel2, seen, 0.0), axis=1, keepdims=True)
    total = carry_ref[...] + jnp.sum(onehot, axis=0, keepdims=True)
    carry_ref[...] = total

    ei_ref[...] = jnp.where(lane == 0, i1, jnp.where(lane == 1, i2, 0.0)).astype(jnp.int32)
    wt_ref[...] = jnp.where(lane == 0, w1, jnp.where(lane == 1, w2, 0.0))
    rk_ref[...] = jnp.where(lane == 0, r1, jnp.where(lane == 1, r2, 0.0)).astype(jnp.int32)
    cnt_ref[...] = jnp.broadcast_to(total, cnt_ref.shape).astype(jnp.int32)


def _router(h, router_w_padded, *, tm):
    m, d = h.shape
    tok = pl.BlockSpec((tm, LANES), lambda i: (i, 0))
    return pl.pallas_call(
        _router_kernel,
        grid=(m // tm,),
        in_specs=[pl.BlockSpec((tm, d), lambda i: (i, 0)),
                  pl.BlockSpec((d, LANES), lambda i: (0, 0))],
        out_specs=[tok, tok, tok, pl.BlockSpec((SUBLANES, LANES), lambda i: (0, 0))],
        out_shape=[jax.ShapeDtypeStruct((m, LANES), jnp.int32), jax.ShapeDtypeStruct((m, LANES), F32),
                   jax.ShapeDtypeStruct((m, LANES), jnp.int32),
                   jax.ShapeDtypeStruct((SUBLANES, LANES), jnp.int32)],
        scratch_shapes=[pltpu.VMEM((1, LANES), F32)],
        compiler_params=_params(("arbitrary",)),
        name="router",
    )(h, router_w_padded)


def _dispatch_kernel(d1_ref, d2_ref, x_ref, buf_in_ref, o_ref, sem, *, td):
    del buf_in_ref
    base = pl.program_id(0) * td

    def row_copy(src_row, dst_row):
        return pltpu.make_async_copy(x_ref.at[pl.ds(src_row, 1), :], o_ref.at[pl.ds(dst_row, 1), :], sem)

    def issue(t, carry):
        row_copy(t, d1_ref[base + t]).start()
        row_copy(t, d2_ref[base + t]).start()
        return carry

    lax.fori_loop(0, td, issue, 0, unroll=8)

    def drain(t, carry):
        row_copy(0, 0).wait()
        row_copy(0, 0).wait()
        return carry

    lax.fori_loop(0, td, drain, 0, unroll=8)


def _dispatch(xp, dst1, dst2, buf, *, td):
    m = xp.shape[0]
    return pl.pallas_call(
        functools.partial(_dispatch_kernel, td=td),
        grid_spec=pltpu.PrefetchScalarGridSpec(
            num_scalar_prefetch=2, grid=(m // td,),
            in_specs=[pl.BlockSpec((td, xp.shape[1]), lambda i, d1, d2: (i, 0)),
                      pl.BlockSpec(memory_space=pl.ANY)],
            out_specs=pl.BlockSpec(memory_space=pl.ANY),
            scratch_shapes=[pltpu.SemaphoreType.DMA(())]),
        out_shape=jax.ShapeDtypeStruct(buf.shape, buf.dtype),
        input_output_aliases={3: 0},
        compiler_params=_params(("arbitrary",)),
        name="moe_dispatch",
    )(dst1, dst2, xp, buf)


def _unpack_halves(xp):
    lo = lax.bitcast_convert_type(xp << jnp.uint32(16), F32).astype(BF16)
    hi = lax.bitcast_convert_type(xp & jnp.uint32(0xFFFF0000), F32).astype(BF16)
    return lo, hi


def _load_group_weights(j, i, sched, w_hbm_refs, wbuf, wb_refs, sems, *, lead, tn, nj):
    te_ref, tf_ref, _, tnx_ref, tlast_ref, tg_ref, ng_ref = sched

    def fetch(e, jj, slot):
        col = pl.multiple_of(jj * tn, tn)
        for m, w in enumerate(w_hbm_refs):
            pltpu.make_async_copy(w.at[lead, e, :, pl.ds(col, tn)], wbuf.at[slot, m], sems.at[slot]).start()

    @pl.when(tf_ref[i] == 1)
    def _():
        g = j * ng_ref[0] + tg_ref[i]
        slot = g % 2

        @pl.when(g == 0)
        def _():
            fetch(te_ref[i], j, slot)

        last = tlast_ref[i] == 1

        @pl.when(jnp.logical_not(last))
        def _():
            fetch(tnx_ref[i], j, 1 - slot)

        @pl.when(jnp.logical_and(last, j + 1 < nj))
        def _():
            fetch(tnx_ref[i], j + 1, 1 - slot)

        for m, w in enumerate(w_hbm_refs):
            pltpu.make_async_copy(w.at[lead, 0, :, pl.ds(0, tn)], wbuf.at[slot, m], sems.at[slot]).wait()
        for m, wb in enumerate(wb_refs):
            wb[...] = wbuf[slot, m].astype(BF16)


def _moe_up_kernel(*refs, lead, tn, nj):
    sched = refs[:7]
    x_ref, wg_hbm, wu_hbm, o_ref, wbuf, wgb_ref, wub_ref, sems = refs[7:]
    tv_ref = sched[2]
    j = pl.program_id(0)
    i = pl.program_id(1)
    _load_group_weights(j, i, sched, (wg_hbm, wu_hbm), wbuf, (wgb_ref, wub_ref), sems, lead=lead, tn=tn, nj=nj)

    @pl.when(tv_ref[i] == 1)
    def _():
        half = x_ref.shape[1]
        lo, hi = _unpack_halves(x_ref[...])
        g = _dot(lo, wgb_ref[pl.ds(0, half), :]) + _dot(hi, wgb_ref[pl.ds(half, half), :])
        u = _dot(lo, wub_ref[pl.ds(0, half), :]) + _dot(hi, wub_ref[pl.ds(half, half), :])
        o_ref[...] = (_silu(g) * u).astype(o_ref.dtype)

    @pl.when(tv_ref[i] == 0)
    def _():
        o_ref[...] = jnp.zeros_like(o_ref)


def _moe_up(xs, wg, wu, lead, sched, *, tn):
    rows, half = xs.shape
    d = 2 * half
    f = wg.shape[3]
    nj = f // tn
    hbm = pl.BlockSpec(memory_space=pl.ANY)
    return pl.pallas_call(
        functools.partial(_moe_up_kernel, lead=lead, tn=tn, nj=nj),
        grid_spec=pltpu.PrefetchScalarGridSpec(
            num_scalar_prefetch=len(sched), grid=(nj, rows // MOE_TR),
            in_specs=[pl.BlockSpec((MOE_TR, half), lambda j, i, *_: (i, 0)), hbm, hbm],
            out_specs=pl.BlockSpec((MOE_TR, tn), lambda j, i, *_: (i, j)),
            scratch_shapes=[pltpu.VMEM((2, 2, d, tn), F32), pltpu.VMEM((d, tn), BF16), pltpu.VMEM((d, tn), BF16),
                            pltpu.SemaphoreType.DMA((2,))]),
        out_shape=jax.ShapeDtypeStruct((rows, f), BF16),
        compiler_params=_params(("arbitrary", "arbitrary")),
        name="moe_up",
    )(*sched, xs, wg, wu)


def _moe_down_kernel(*refs, lead, tn, nj):
    sched = refs[:7]
    x_ref, w_hbm, o_ref, wbuf, wb_ref, sems = refs[7:]
    tv_ref = sched[2]
    j = pl.program_id(0)
    i = pl.program_id(1)
    _load_group_weights(j, i, sched, (w_hbm,), wbuf, (wb_ref,), sems, lead=lead, tn=tn, nj=nj)

    @pl.when(tv_ref[i] == 1)
    def _():
        o_ref[...] = _dot(x_ref[...], wb_ref[...])

    @pl.when(tv_ref[i] == 0)
    def _():
        o_ref[...] = jnp.zeros_like(o_ref)


def _moe_down(hid, wd, lead, sched, *, tn):
    rows, f = hid.shape
    d = wd.shape[3]
    nj = d // tn
    return pl.pallas_call(
        functools.partial(_moe_down_kernel, lead=lead, tn=tn, nj=nj),
        grid_spec=pltpu.PrefetchScalarGridSpec(
            num_scalar_prefetch=len(sched), grid=(nj, rows // MOE_TR),
            in_specs=[pl.BlockSpec((MOE_TR, f), lambda j, i, *_: (i, 0)), pl.BlockSpec(memory_space=pl.ANY)],
            out_specs=pl.BlockSpec((MOE_TR, tn), lambda j, i, *_: (i, j)),
            scratch_shapes=[pltpu.VMEM((2, 1, f, tn), F32), pltpu.VMEM((f, tn), BF16),
                            pltpu.SemaphoreType.DMA((2,))]),
        out_shape=jax.ShapeDtypeStruct((rows, d), F32),
        compiler_params=_params(("arbitrary", "arbitrary")),
        name="moe_down",
    )(*sched, hid, wd)


def _combine_ln_kernel(d1_ref, d2_ref, y_ref, h_ref, wt_ref, g_ref, b_ref, o_ref, ob_ref,
                       buf1, buf2, sems, *, alpha, tc):
    i = pl.program_id(0)
    slot = i % 2

    def row_copy(src_row, buf, s, t):
        return pltpu.make_async_copy(y_ref.at[pl.ds(src_row, 1), :], buf.at[s, pl.ds(t, 1), :], sems.at[s])

    def gather_tile(step, s):
        base = step * tc

        def issue(t, carry):
            row_copy(d1_ref[base + t], buf1, s, t).start()
            row_copy(d2_ref[base + t], buf2, s, t).start()
            return carry

        lax.fori_loop(0, tc, issue, 0, unroll=8)

    @pl.when(i == 0)
    def _():
        gather_tile(0, 0)

    @pl.when(i + 1 < pl.num_programs(0))
    def _():
        gather_tile(i + 1, 1 - slot)

    def drain(t, carry):
        row_copy(0, buf1, slot, 0).wait()
        row_copy(0, buf2, slot, 0).wait()
        return carry

    lax.fori_loop(0, tc, drain, 0, unroll=8)

    wt = wt_ref[...]
    x = alpha * h_ref[...] + wt[:, 0:1] * buf1[slot] + wt[:, 1:2] * buf2[slot]
    mu = jnp.mean(x, axis=-1, keepdims=True)
    xc = x - mu
    var = jnp.mean(xc * xc, axis=-1, keepdims=True)
    out = xc * lax.rsqrt(var + LN_EPS) * g_ref[...] + b_ref[...]
    o_ref[...] = out
    ob_ref[...] = out.astype(BF16)


def _combine_ln(y, dst1, dst2, h, wt, g, b, *, alpha, tc):
    m, d = h.shape
    row = pl.BlockSpec((tc, d), lambda i, d1, d2: (i, 0))
    vec = pl.BlockSpec((1, d), lambda i, d1, d2: (0, 0))
    return pl.pallas_call(
        functools.partial(_combine_ln_kernel, alpha=alpha, tc=tc),
        grid_spec=pltpu.PrefetchScalarGridSpec(
            num_scalar_prefetch=2, grid=(m // tc,),
            in_specs=[pl.BlockSpec(memory_space=pl.ANY), row,
                      pl.BlockSpec((tc, LANES), lambda i, d1, d2: (i, 0)), vec, vec],
            out_specs=[row, row],
            scratch_shapes=[pltpu.VMEM((2, tc, d), F32), pltpu.VMEM((2, tc, d), F32),
                            pltpu.SemaphoreType.DMA((2,))]),
        out_shape=[jax.ShapeDtypeStruct((m, d), F32), jax.ShapeDtypeStruct((m, d), BF16)],
        compiler_params=_params(("arbitrary",)),
        name="moe_combine_ln",
    )(dst1, dst2, y, h, wt, g.reshape(1, d), b.reshape(1, d))


def _add_ln_kernel(h_ref, y_ref, g_ref, b_ref, o_ref, ob_ref, *maybe_packed_ref, alpha):
    x = alpha * h_ref[...] + y_ref[...]
    mu = jnp.mean(x, axis=-1, keepdims=True)
    xc = x - mu
    var = jnp.mean(xc * xc, axis=-1, keepdims=True)
    out = xc * lax.rsqrt(var + LN_EPS) * g_ref[...] + b_ref[...]
    o_ref[...] = out
    out_b = out.astype(BF16)
    ob_ref[...] = out_b
    if maybe_packed_ref:
        half = out.shape[1] // 2
        as_u32 = lax.bitcast_convert_type(out_b.astype(F32), jnp.uint32)
        maybe_packed_ref[0][...] = (as_u32[:, :half] >> jnp.uint32(16)) | as_u32[:, half:]


def _add_ln(h, y, g, b, *, alpha, tm, packed=False):
    m, d = h.shape
    row = pl.BlockSpec((tm, d), lambda i: (i, 0))
    vec = pl.BlockSpec((1, d), lambda i: (0, 0))
    out_specs = [row, row]
    out_shape = [jax.ShapeDtypeStruct((m, d), F32), jax.ShapeDtypeStruct((m, d), BF16)]
    if packed:
        out_specs.append(pl.BlockSpec((tm, d // 2), lambda i: (i, 0)))
        out_shape.append(jax.ShapeDtypeStruct((m, d // 2), jnp.uint32))
    return pl.pallas_call(
        functools.partial(_add_ln_kernel, alpha=alpha),
        grid=(m // tm,),
        in_specs=[row, row, vec, vec],
        out_specs=out_specs,
        out_shape=out_shape,
        compiler_params=_params(("parallel",)),
        name="add_ln",
    )(h, y, g.reshape(1, d), b.reshape(1, d))


def _level_masks(c):
    t = lax.broadcasted_iota(jnp.int32, (c, c), 0)
    s = lax.broadcasted_iota(jnp.int32, (c, c), 1)
    x = t ^ s
    masks = [t == s]
    for p in range(1, int(math.log2(c)) + 1):
        masks.append((t > s) & ((x >> (p - 1)) == 1))
    return masks


def _midpoint_rows(cum_ref, c, k, p):
    blk = 1 << p
    half = blk >> 1
    if blk >= SUBLANES:
        parts = [jnp.broadcast_to(cum_ref[pl.ds(b * blk + half - 1, 1), :], (blk, k))
                 for b in range(c // blk)]
        return parts[0] if len(parts) == 1 else jnp.concatenate(parts, axis=0)
    sub = lax.broadcasted_iota(jnp.int32, (c, k), 0) & (SUBLANES - 1)
    res = None
    for j in range(SUBLANES // blk):
        rows = jnp.concatenate(
            [jnp.broadcast_to(cum_ref[pl.ds(SUBLANES * v + j * blk + half - 1, 1), :], (SUBLANES, k))
             for v in range(c // SUBLANES)], axis=0)
        res = rows if res is None else jnp.where(sub >= j * blk, rows, res)
    return res


def _split3(x):
    hi = x.astype(BF16)
    r1 = x - hi.astype(F32)
    mid = r1.astype(BF16)
    lo = (r1 - mid.astype(F32)).astype(BF16)
    return hi, mid, lo


def _gla_chunk(q, k, v, g, state_ref, cum_ref, masks, tri):
    c, kd = q.shape
    g_hi, g_mid, g_lo = _split3(g)
    cum = _dot(tri, g_hi) + _dot(tri, g_mid) + _dot(tri, g_lo)
    cum_ref[...] = cum
    row = lax.broadcasted_iota(jnp.int32, (c, kd), 0)
    qb = q.astype(BF16)
    kb = k.astype(BF16)
    vb = v.astype(BF16)
    scores = jnp.where(masks[0], _dot_nt(qb, kb), 0.0)
    for p in range(1, len(masks)):
        mid = _midpoint_rows(cum_ref, c, kd, p)
        upper = ((row >> (p - 1)) & 1) == 1
        z = (jnp.where(upper, q, k) * jnp.exp(-jnp.abs(cum - mid))).astype(BF16)
        scores = jnp.where(masks[p], _dot_nt(z, z), scores)
    state = state_ref[...]
    out = _dot(scores.astype(BF16), vb)
    out = out + _dot_nt((q * jnp.exp(cum)).astype(BF16), state.astype(BF16))
    last = cum_ref[pl.ds(c - 1, 1), :]
    k_dec = (k * jnp.exp(last - cum)).astype(BF16)
    state_ref[...] = state * jnp.exp(last) + _dot_tn(vb, k_dec)
    return out


def _tri_ones(c):
    t = lax.broadcasted_iota(jnp.int32, (c, c), 0)
    s = lax.broadcasted_iota(jnp.int32, (c, c), 1)
    return jnp.where(t >= s, 1.0, 0.0).astype(BF16)


def _hgrn_kernel(q_ref, f_ref, i_ref, g_ref, lb_ref, nw_ref, o_ref, state_ref, cum_ref, *, layer, chunk):
    @pl.when(pl.program_id(2) == 0)
    def _():
        state_ref[...] = jnp.zeros_like(state_ref)

    lb_all = lb_ref[...]
    e = jnp.exp(lb_all - jnp.max(lb_all, axis=0, keepdims=True))
    soft = e / jnp.sum(e, axis=0, keepdims=True)
    lower = jnp.zeros((1, LANES), F32)
    for l in range(1, layer + 1):
        lower = lower + soft[l:l + 1, :]
    masks = _level_masks(chunk)
    tri = _tri_ones(chunk)
    kd = q_ref.shape[1]
    for c in range(q_ref.shape[0] // chunk):
        rows = pl.ds(c * chunk, chunk)
        f = lower + (1.0 - lower) * _sigmoid(f_ref[rows, :])
        q = _silu(q_ref[rows, :]) * (kd ** -0.5)
        o = _gla_chunk(q, 1.0 - f, i_ref[rows, :], jnp.log(f), state_ref, cum_ref, masks, tri)
        o = o * lax.rsqrt(jnp.mean(o * o, axis=-1, keepdims=True) + RMS_EPS) * nw_ref[...]
        o_ref[rows, :] = (o * _silu(g_ref[rows, :])).astype(o_ref.dtype)


def _hgrn(proj, hgrn_lb, norm_w, *, layer, batch, seq):
    nt = seq // MIX_TT

    def col(cb):
        return pl.BlockSpec((MIX_TT, LANES), lambda b, h, t: (b * nt + t, cb + h))

    return pl.pallas_call(
        functools.partial(_hgrn_kernel, layer=layer, chunk=HGRN_CHUNK),
        grid=(batch, HGRN_HEADS, nt),
        in_specs=[col(CB_A_Q), col(CB_A_F), col(CB_A_I), col(CB_A_G),
                  pl.BlockSpec((hgrn_lb.shape[0], LANES), lambda b, h, t: (0, h)),
                  pl.BlockSpec((1, LANES), lambda b, h, t: (0, 0))],
        out_specs=pl.BlockSpec((MIX_TT, LANES), lambda b, h, t: (b * nt + t, h)),
        out_shape=jax.ShapeDtypeStruct((batch * seq, GROUP_WIDTH), BF16),
        scratch_shapes=[pltpu.VMEM((LANES, LANES), F32), pltpu.VMEM((HGRN_CHUNK, LANES), F32)],
        compiler_params=_params(("parallel", "parallel", "arbitrary")),
        name="hgrn",
    )(proj, proj, proj, proj, hgrn_lb, norm_w.reshape(1, LANES))


def _gla_kernel(q_ref, k_ref, v_ref, g_ref, r_ref, wg_ref, bg_ref, nw_ref, o_ref, state_ref, cum_ref, *, chunk):
    @pl.when(pl.program_id(2) == 0)
    def _():
        state_ref[...] = jnp.zeros_like(state_ref)

    masks = _level_masks(chunk)
    tri = _tri_ones(chunk)
    kd = q_ref.shape[1]
    for c in range(q_ref.shape[0] // chunk):
        rows = pl.ds(c * chunk, chunk)
        pre = _dot(r_ref[rows, :], wg_ref[...]) + bg_ref[...]
        log_alpha = (jnp.minimum(pre, 0.0) - jnp.log1p(jnp.exp(-jnp.abs(pre)))) * (1.0 / GLA_NORMALIZER)
        o = _gla_chunk(q_ref[rows, :] * (kd ** -0.5), k_ref[rows, :], v_ref[rows, :], log_alpha,
                       state_ref, cum_ref, masks, tri)
        o = o * lax.rsqrt(jnp.mean(o * o, axis=-1, keepdims=True) + RMS_EPS) * nw_ref[...]
        o_ref[rows, :] = (o * _silu(g_ref[rows, :])).astype(o_ref.dtype)


def _gla(proj, proj_r, w_gate_padded, b_gate, norm_w, *, batch, seq):
    nt = seq // MIX_TT
    vb = GLA_VAL // LANES

    return pl.pallas_call(
        functools.partial(_gla_kernel, chunk=GLA_CHUNK),
        grid=(batch, GLA_HEADS, nt),
        in_specs=[pl.BlockSpec((MIX_TT, GLA_KEY), lambda b, h, t: (b * nt + t, CB_C_Q + h)),
                  pl.BlockSpec((MIX_TT, GLA_KEY), lambda b, h, t: (b * nt + t, CB_C_K + h)),
                  pl.BlockSpec((MIX_TT, GLA_VAL), lambda b, h, t: (b * nt + t, CB_C_V // vb + h)),
                  pl.BlockSpec((MIX_TT, GLA_VAL), lambda b, h, t: (b * nt + t, CB_C_G // vb + h)),
                  pl.BlockSpec((MIX_TT, LANES), lambda b, h, t: (b * nt + t, 0)),
                  pl.BlockSpec((LANES, GLA_KEY), lambda b, h, t: (0, h)),
                  pl.BlockSpec((1, GLA_KEY), lambda b, h, t: (0, h)),
                  pl.BlockSpec((1, GLA_VAL), lambda b, h, t: (0, 0))],
        out_specs=pl.BlockSpec((MIX_TT, GLA_VAL), lambda b, h, t: (b * nt + t, h)),
        out_shape=jax.ShapeDtypeStruct((batch * seq, GROUP_WIDTH), BF16),
        scratch_shapes=[pltpu.VMEM((GLA_VAL, GLA_KEY), F32), pltpu.VMEM((GLA_CHUNK, GLA_KEY), F32)],
        compiler_params=_params(("parallel", "parallel", "arbitrary")),
        name="gla",
    )(proj, proj, proj, proj, proj_r, w_gate_padded, b_gate.reshape(1, -1), norm_w.reshape(1, GLA_VAL))


def _ret_kernel(q_ref, k_ref, v_ref, g_ref, cos_ref, sin_ref, dmat_ref, qdec_ref, kdec_ref, cdec_ref, nw_ref,
                o_ref, state_ref, *, chunk):
    @pl.when(pl.program_id(2) == 0)
    def _():
        state_ref[...] = jnp.zeros_like(state_ref)

    kd = q_ref.shape[1]
    half = kd // 2
    dmat = dmat_ref[...]
    qdec = qdec_ref[...]
    kdec = kdec_ref[...]
    cdec = cdec_ref[...]
    for c in range(q_ref.shape[0] // chunk):
        rows = pl.ds(c * chunk, chunk)
        cos = cos_ref[rows, :]
        sin = sin_ref[rows, :]
        q = q_ref[rows, :]
        k = k_ref[rows, :]
        q = q * cos + pltpu.roll(q, half, 1) * sin
        k = (k * cos + pltpu.roll(k, half, 1) * sin) * (kd ** -0.5)
        vb = v_ref[rows, :].astype(BF16)
        state = state_ref[...]
        scores = _dot_nt(q.astype(BF16), k.astype(BF16)) * dmat
        o = _dot(scores.astype(BF16), vb) + _dot_nt((q * qdec).astype(BF16), state.astype(BF16))
        state_ref[...] = state * cdec + _dot_tn(vb, (k * kdec).astype(BF16))
        mu = jnp.mean(o, axis=-1, keepdims=True)
        oc = o - mu
        var = jnp.mean(oc * oc, axis=-1, keepdims=True)
        o = oc * lax.rsqrt(var + LN_EPS) * nw_ref[...]
        o_ref[rows, :] = (o * _silu(g_ref[rows, :])).astype(o_ref.dtype)


def _retention_tables(seq, chunk):
    half = LANES // 2
    inv_freq = ROPE_BASE ** (-jnp.arange(half, dtype=F32) / half)
    ang = jnp.arange(seq, dtype=F32)[:, None] * inv_freq[None, :]
    cos, sin = jnp.cos(ang), jnp.sin(ang)
    cos2 = jnp.concatenate([cos, cos], axis=1)
    sin2 = jnp.concatenate([-sin, sin], axis=1)
    log_gamma = jnp.log1p(-(2.0 ** (-5.0 - jnp.arange(RET_HEADS, dtype=F32))))
    pos = jnp.arange(chunk, dtype=F32)
    rel = pos[:, None] - pos[None, :]
    dmat = jnp.where(rel >= 0, jnp.exp(log_gamma[:, None, None] * jnp.maximum(rel, 0.0)), 0.0)
    qdec = jnp.exp(log_gamma[:, None] * (pos + 1.0)[None, :])
    kdec = jnp.exp(log_gamma[:, None] * (chunk - 1.0 - pos)[None, :])
    cdec = jnp.exp(log_gamma * chunk)
    bcast = lambda a: jnp.broadcast_to(a[..., None], a.shape + (LANES,))
    return cos2, sin2, dmat, bcast(qdec), bcast(kdec), bcast(cdec[:, None])


def _retention(proj, tables, norm_w, *, batch, seq):
    nt = seq // MIX_TT
    cos2, sin2, dmat, qdec, kdec, cdec = tables
    chunk = dmat.shape[1]

    def col(cb):
        return pl.BlockSpec((MIX_TT, LANES), lambda b, h, t: (b * nt + t, cb + h))

    rot = pl.BlockSpec((MIX_TT, LANES), lambda b, h, t: (t, 0))
    return pl.pallas_call(
        functools.partial(_ret_kernel, chunk=chunk),
        grid=(batch, RET_HEADS, nt),
        in_specs=[col(CB_D_Q), col(CB_D_K), col(CB_D_V), col(CB_D_G), rot, rot,
                  pl.BlockSpec((None, chunk, chunk), lambda b, h, t: (h, 0, 0)),
                  pl.BlockSpec((None, chunk, LANES), lambda b, h, t: (h, 0, 0)),
                  pl.BlockSpec((None, chunk, LANES), lambda b, h, t: (h, 0, 0)),
                  pl.BlockSpec((None, 1, LANES), lambda b, h, t: (h, 0, 0)),
                  pl.BlockSpec((1, LANES), lambda b, h, t: (0, 0))],
        out_specs=pl.BlockSpec((MIX_TT, LANES), lambda b, h, t: (b * nt + t, h)),
        out_shape=jax.ShapeDtypeStruct((batch * seq, GROUP_WIDTH), BF16),
        scratch_shapes=[pltpu.VMEM((LANES, LANES), F32)],
        compiler_params=_params(("parallel", "parallel", "arbitrary")),
        name="retention",
    )(proj, proj, proj, proj, cos2, sin2, dmat, qdec, kdec, cdec, norm_w.reshape(1, LANES))


def _lru_kernel(x_ref, y_ref, cw_ref, cb_ref, wa_ref, ba_ref, wx_ref, bx_ref, lam_ref, o_ref,
                xbuf_ref, a_ref, u_ref, carry_ref, *, scan_chunk):
    tt = x_ref.shape[0]

    @pl.when(pl.program_id(2) == 0)
    def _():
        xbuf_ref[pl.ds(0, SUBLANES), :] = jnp.zeros((SUBLANES, LANES), F32)
        carry_ref[...] = jnp.zeros_like(carry_ref)

    xbuf_ref[pl.ds(SUBLANES, tt), :] = x_ref[...]
    xc = cb_ref[...]
    for j in range(CONV_WIDTH):
        xc = xc + cw_ref[pl.ds(j, 1), :] * xbuf_ref[pl.ds(SUBLANES - (CONV_WIDTH - 1) + j, tt), :]
    xbuf_ref[pl.ds(0, SUBLANES), :] = x_ref[pl.ds(tt - SUBLANES, SUBLANES), :]

    xcb = xc.astype(BF16)
    r = _sigmoid_tanh(_dot(xcb, wa_ref[...].astype(BF16)) + ba_ref[...])
    i = _sigmoid_tanh(_dot(xcb, wx_ref[...].astype(BF16)) + bx_ref[...])
    log_a = (-LRU_C) * r * _softplus(-lam_ref[...])
    a_ref[...] = jnp.exp(log_a)
    th = jnp.tanh(log_a)
    u_ref[...] = jnp.sqrt(-2.0 * th / (1.0 - th)) * (i * xc)

    row = lax.broadcasted_iota(jnp.int32, (scan_chunk, LANES), 0)
    for c in range(tt // scan_chunk):
        rows = pl.ds(c * scan_chunk, scan_chunk)
        a = a_ref[rows, :]
        u = u_ref[rows, :]
        shift = 1
        while shift < scan_chunk:
            valid = row >= shift
            u = jnp.where(valid, a * pltpu.roll(u, shift, 0) + u, u)
            a = jnp.where(valid, a * pltpu.roll(a, shift, 0), a)
            shift *= 2
        h = a * carry_ref[...] + u
        carry_ref[...] = h[scan_chunk - 1:scan_chunk, :]
        o_ref[rows, :] = (h * jax.nn.gelu(y_ref[rows, :])).astype(o_ref.dtype)


def _rg_lru(proj, conv_w, conv_b, wa, ba, wx, bx, lam, *, batch, seq):
    nt = seq // MIX_TT
    vec = pl.BlockSpec((1, LANES), lambda b, n, t: (0, n))
    mat = pl.BlockSpec((None, LANES, LANES), lambda b, n, t: (n, 0, 0))
    return pl.pallas_call(
        functools.partial(_lru_kernel, scan_chunk=SCAN_CHUNK),
        grid=(batch, LRU_BLOCKS, nt),
        in_specs=[pl.BlockSpec((MIX_TT, LANES), lambda b, n, t: (b * nt + t, CB_B_X + n)),
                  pl.BlockSpec((MIX_TT, LANES), lambda b, n, t: (b * nt + t, CB_B_Y + n)),
                  pl.BlockSpec((CONV_WIDTH, LANES), lambda b, n, t: (0, n)),
                  vec, mat, vec, mat, vec, vec],
        out_specs=pl.BlockSpec((MIX_TT, LANES), lambda b, n, t: (b * nt + t, n)),
        out_shape=jax.ShapeDtypeStruct((batch * seq, GROUP_WIDTH), BF16),
        scratch_shapes=[pltpu.VMEM((MIX_TT + SUBLANES, LANES), F32),
                        pltpu.VMEM((MIX_TT, LANES), F32),
                        pltpu.VMEM((MIX_TT, LANES), F32),
                        pltpu.VMEM((1, LANES), F32)],
        compiler_params=_params(("parallel", "parallel", "arbitrary")),
        name="rg_lru",
    )(proj, proj, conv_w, conv_b.reshape(1, -1), wa, ba.reshape(1, -1), wx, bx.reshape(1, -1),
      lam.reshape(1, -1))


def _moe_ffn_ln(h, hp, sorted_buf, router_w, exp_w_gate, exp_w_up, exp_w_down, lead, ln_g, ln_b, *, alpha):
    m, d = h.shape
    rw = jnp.concatenate([router_w, jnp.zeros((d, LANES - N_EXPERTS), F32)], axis=1)
    ei, wt, rk, cnt = _router(h, rw, tm=256)

    counts = cnt[0, :N_EXPERTS]
    padded = ((counts + MOE_TR - 1) // MOE_TR) * MOE_TR
    ends = jnp.cumsum(padded)
    offs = ends - padded
    dst1 = offs[ei[:, 0]] + rk[:, 0]
    dst2 = offs[ei[:, 1]] + rk[:, 1]
    starts = jnp.arange(sorted_buf.shape[0] // MOE_TR, dtype=jnp.int32) * MOE_TR
    te = jnp.minimum(jnp.sum(starts[:, None] >= ends[None, :], axis=1), N_EXPERTS - 1).astype(jnp.int32)
    tv = (starts < ends[N_EXPERTS - 1]).astype(jnp.int32)
    prev = jnp.concatenate([jnp.full((1,), -1, jnp.int32), te[:-1]])
    tf = (te != prev).astype(jnp.int32) * tv
    n_tiles = starts.shape[0]
    tg = (jnp.cumsum(tf) - 1).astype(jnp.int32)
    ng = jnp.sum(tf).astype(jnp.int32).reshape(1)
    group_start = jnp.where(tf == 1, jnp.arange(n_tiles, dtype=jnp.int32), n_tiles)
    next_start = lax.cummin(group_start, axis=0, reverse=True)
    next_start = jnp.concatenate([next_start[1:], jnp.full((1,), n_tiles, jnp.int32)])
    tlast = (next_start >= n_tiles).astype(jnp.int32)
    tnx = jnp.where(tlast == 1, te[0], te[jnp.minimum(next_start, n_tiles - 1)]).astype(jnp.int32)
    sched = (te, tf, tv, tnx, tlast, tg, ng)

    sorted_buf = _dispatch(hp, dst1, dst2, sorted_buf, td=256)
    hid = _moe_up(sorted_buf, exp_w_gate, exp_w_up, lead, sched, tn=512)
    y = _moe_down(hid, exp_w_down, lead, sched, tn=2048)
    h, hb = _combine_ln(y, dst1, dst2, h, wt, ln_g, ln_b, alpha=alpha, tc=256)
    return h, hb, sorted_buf


def kernel(x, w_in, w_out, hgrn_lb, hgrn_norm, conv_w, conv_b, lru_wa, lru_ba, lru_wx, lru_bx, lru_lambda,
           gla_w_gate, gla_b_gate, gla_norm, ret_norm, ln1_g, ln1_b, ln2_g, ln2_b, ffn_w_gate, ffn_w_up,
           ffn_w_down, router_w, exp_w_gate, exp_w_up, exp_w_down):
    batch, seq, d = x.shape
    depth = w_in.shape[0]
    m = batch * seq
    alpha = (2 * depth) ** 0.25
    tables = _retention_tables(seq, RET_CHUNK)

    sorted_rows = TOP_K * m + N_EXPERTS * MOE_TR
    sorted_buf = jnp.zeros((sorted_rows, d // 2), jnp.uint32)

    w_in_t = jnp.swapaxes(w_in, 1, 2)
    w_r = jnp.pad(w_in_t[:, ABC_COLS:D_COL0, :], ((0, 0), (0, LANES - GLA_RANK), (0, 0)))

    h = x.reshape(m, d)
    hb = h.astype(BF16)
    for layer in range(depth):
        proj = _matmul_nt(hb, w_in_t, layer, row0=0, n=ABC_COLS, tm=1024, tn=512, out_dtype=F32, name="in_proj")
        proj_d = _matmul_nt(hb, w_in_t, layer, row0=D_COL0, n=d, tm=1024, tn=512, out_dtype=F32,
                            name="in_proj_ret")
        proj_r = _matmul_nt(hb, w_r, layer, row0=0, n=LANES, tm=1024, tn=LANES, out_dtype=F32,
                            name="in_proj_gate")

        w_gate_p = jnp.concatenate(
            [gla_w_gate[layer], jnp.zeros((LANES - GLA_RANK, gla_w_gate.shape[2]), F32)], axis=0)
        o_a = _hgrn(proj, hgrn_lb, hgrn_norm[layer], layer=layer, batch=batch, seq=seq)
        o_b = _rg_lru(proj, conv_w[layer], conv_b[layer], lru_wa[layer], lru_ba[layer], lru_wx[layer],
                      lru_bx[layer], lru_lambda[layer], batch=batch, seq=seq)
        o_c = _gla(proj, proj_r, w_gate_p, gla_b_gate[layer], gla_norm[layer], batch=batch, seq=seq)
        o_d = _retention(proj_d, tables, ret_norm[layer], batch=batch, seq=seq)
        mix = _matmul_wres([o_a, o_b, o_c, o_d], w_out, layer, n_cols=d, tm=1024, tn=512, out_dtype=BF16,
                           name="out_proj")

        j = layer // 2
        if layer % 2 == 0:
            h, hb = _add_ln(h, mix, ln1_g[layer], ln1_b[layer], alpha=alpha, tm=256)
            hid = _swiglu_up(hb, ffn_w_gate, ffn_w_up, j, tm=1024, tn=256)
            ff = _matmul_kslab(hid, ffn_w_down, j, tm=1024, tn=256, slab=2048, out_dtype=BF16, name="ffn_down")
            h, hb = _add_ln(h, ff, ln2_g[layer], ln2_b[layer], alpha=alpha, tm=256)
        else:
            h, hb, hp = _add_ln(h, mix, ln1_g[layer], ln1_b[layer], alpha=alpha, tm=256, packed=True)
            h, hb, sorted_buf = _moe_ffn_ln(h, hp, sorted_buf, router_w[j], exp_w_gate, exp_w_up, exp_w_down, j,
                                            ln2_g[layer], ln2_b[layer], alpha=alpha)
    return h.reshape(batch, seq, d)
```
